```python
import jax
import jax.numpy as jnp
from jax import lax
import numpy as np

D_MODEL = 2048
BATCH = 16
SEQ = 256
DEPTH = 4
DEC_BATCH = 4
DEC_SEQ = 4096
PAST_LEN = 256

GRID_W = 64
N_MIXERS = 3
N_A_LAYERS = (DEPTH + 2) // 3
N_B_LAYERS = (DEPTH + 1) // 3
N_C_LAYERS = DEPTH // 3
D_FF = 4 * D_MODEL
N_MOD = 6
NORM_EPS = 1e-6
ROPE_BASE = 10000.0
Q_BLOCK = 128
NEG_INF = -1e30

HGRN_HEAD_K = 128
HGRN_HEADS = D_MODEL // HGRN_HEAD_K
HGRN_HEAD_V = D_MODEL // HGRN_HEADS
HGRN_KW = HGRN_HEADS * HGRN_HEAD_K
HGRN_VW = HGRN_HEADS * HGRN_HEAD_V
HGRN_IN = 3 * HGRN_KW + 2 * HGRN_VW
HGRN_CHUNK = 32

MLA_HEADS = 16
MLA_Q_LORA = 512
MLA_KV_LORA = 512
MLA_NOPE = 128
MLA_ROPE = 64
MLA_V = 128
MLA_QK = MLA_NOPE + MLA_ROPE
MLA_DOWN = MLA_Q_LORA + MLA_KV_LORA + MLA_ROPE

SWA_HEAD_DIM = 64
SWA_Q_HEADS = D_MODEL // SWA_HEAD_DIM
SWA_KV_HEADS = 8
SWA_WINDOW = 128
SWA_BLOCK = 128
SWA_QW = SWA_Q_HEADS * SWA_HEAD_DIM
SWA_KVW = SWA_KV_HEADS * SWA_HEAD_DIM

kernel_name = 'hybrid_diffusion_trunk_step'


def rmsnorm(x, g):
    xf = x.astype(jnp.float32)
    y = xf * lax.rsqrt(jnp.mean(xf * xf, axis=-1, keepdims=True) + NORM_EPS)
    return (y * g.astype(jnp.float32)).astype(x.dtype)


def grid_positions(n_tokens):
    rows = n_tokens // GRID_W
    row = jnp.repeat(jnp.arange(rows, dtype=jnp.int32), GRID_W)
    col = jnp.tile(jnp.arange(GRID_W, dtype=jnp.int32), rows)
    return row, col


def axial_rope(x, row, col):
    r = x.shape[-1]
    half = r // 2
    quarter = half // 2
    inv = ROPE_BASE ** (-jnp.arange(quarter, dtype=jnp.float32) / quarter)

    def rot(xa, pos):
        ang = pos.astype(jnp.float32)[:, None] * inv[None, :]
        cos = jnp.cos(ang)[None, :, None, :]
        sin = jnp.sin(ang)[None, :, None, :]
        xa = xa.astype(jnp.float32)
        x1, x2 = xa[..., :quarter], xa[..., quarter:]
        return jnp.concatenate([x1 * cos - x2 * sin, x2 * cos + x1 * sin], axis=-1)

    out = jnp.concatenate([rot(x[..., :half], row), rot(x[..., half:], col)], axis=-1)
    return out.astype(x.dtype)


def adaln_params(cond, w, b):
    m = jax.nn.silu(cond) @ w + b
    return m.reshape(cond.shape[0], 1, N_MOD, D_MODEL)


def pre_sublayer(y, g, m, k):
    return rmsnorm(y, g) * (1 + m[:, :, k + 1]) + m[:, :, k]


def post_sublayer(y, out, g, m, k):
    return y + m[:, :, k + 2] * rmsnorm(out, g)


def softmax_with_sink(s, sink):
    if sink is None:
        return jax.nn.softmax(s, axis=-1)
    col = jnp.broadcast_to(sink.astype(jnp.float32)[None, :, :, None, None], s.shape[:-1] + (1,))
    return jax.nn.softmax(jnp.concatenate([s, col], axis=-1), axis=-1)[..., :-1]


def dense_attention(q, k, v, scale, sink):
    B, Lq, Hq, Dk = q.shape
    Hkv, Dv = k.shape[2], v.shape[-1]
    G = Hq // Hkv
    nq = Lq // Q_BLOCK
    qb = q.reshape(B, nq, Q_BLOCK, Hkv, G, Dk).transpose(1, 0, 2, 3, 4, 5)
    sink_g = None if sink is None else sink.reshape(Hkv, G)

    def block(qblk):
        s = jnp.einsum('bqkgd,bskd->bkgqs', qblk, k).astype(jnp.float32) * scale
        p = softmax_with_sink(s, sink_g).astype(v.dtype)
        return jnp.einsum('bkgqs,bskd->bqkgd', p, v)

    o = lax.map(block, qb)
    return o.transpose(1, 0, 2, 3, 4, 5).reshape(B, Lq, Hq, Dv)


def window_attention(q, k, v, k_ctx, v_ctx, sink, scale):
    B, L, Hq, Dh = q.shape
    Hkv, Dv = k.shape[2], v.shape[-1]
    G = Hq // Hkv
    nb = L // SWA_BLOCK
    n_ctx = k_ctx.shape[1]
    pad = ((0, 0), (SWA_BLOCK, SWA_BLOCK), (0, 0), (0, 0))
    kp = jnp.pad(k, pad)
    vp = jnp.pad(v, pad)
    qb = q.reshape(B, nb, SWA_BLOCK, Hkv, G, Dh).transpose(1, 0, 2, 3, 4, 5)
    sink_g = sink.reshape(Hkv, G)

    def block(args):
        j, qblk = args
        start = j * SWA_BLOCK
        kb = lax.dynamic_slice_in_dim(kp, start, 3 * SWA_BLOCK, axis=1)
        vb = lax.dynamic_slice_in_dim(vp, start, 3 * SWA_BLOCK, axis=1)
        qi = start + jnp.arange(SWA_BLOCK)
        ki = start - SWA_BLOCK + jnp.arange(3 * SWA_BLOCK)
        keep = (jnp.abs(qi[:, None] - ki[None, :]) <= SWA_WINDOW) & (ki[None, :] >= 0) & (ki[None, :] < L)
        s_loc = jnp.einsum('bqkgd,bskd->bkgqs', qblk, kb).astype(jnp.float32) * scale
        s_loc = jnp.where(keep, s_loc, NEG_INF)
        s_ctx = jnp.einsum('bqkgd,bskd->bkgqs', qblk, k_ctx).astype(jnp.float32) * scale
        p = softmax_with_sink(jnp.concatenate([s_ctx, s_loc], axis=-1), sink_g).astype(v.dtype)
        return (jnp.einsum('bkgqs,bskd->bqkgd', p[..., :n_ctx], v_ctx)
                + jnp.einsum('bkgqs,bskd->bqkgd', p[..., n_ctx:], vb))

    o = lax.map(block, (jnp.arange(nb), qb))
    return o.transpose(1, 0, 2, 3, 4, 5).reshape(B, L, Hq, Dv)


def gla_scan(q, k, v, log_f, s0):
    B, L, H, K = q.shape
    V = v.shape[-1]
    n = L // HGRN_CHUNK

    def chunks(a):
        return a.reshape(B, n, HGRN_CHUNK, H, a.shape[-1]).transpose(1, 0, 3, 2, 4)

    causal = jnp.tril(jnp.ones((HGRN_CHUNK, HGRN_CHUNK), dtype=bool))[:, :, None]

    def step(s, blk):
        qc, kc, vc, gc = blk
        b = jnp.cumsum(gc, axis=2)
        o_inter = jnp.einsum('bhtk,bhkv->bhtv', qc * jnp.exp(b), s)
        decay = jnp.exp(jnp.where(causal, b[:, :, :, None, :] - b[:, :, None, :, :], -jnp.inf))
        attn = jnp.einsum('bhtk,bhsk,bhtsk->bhts', qc, kc, decay)
        o_intra = jnp.einsum('bhts,bhsv->bhtv', attn, vc)
        b_end = b[:, :, -1:, :]
        s_new = (jnp.exp(b_end[:, :, 0, :])[..., None] * s
                 + jnp.einsum('bhsk,bhsv->bhkv', kc * jnp.exp(b_end - b), vc))
        return s_new, o_inter + o_intra

    s_fin, o = lax.scan(step, s0, (chunks(q), chunks(k), chunks(v), chunks(log_f)))
    return o.transpose(1, 0, 3, 2, 4).reshape(B, L, H, V), s_fin


def hgrn_mix(h, s_init, w_in, lb, norm_g, w_out):
    B, L, _ = h.shape
    p = h @ w_in
    kw, vw = HGRN_KW, HGRN_VW
    q = jax.nn.silu(p[..., :kw].astype(jnp.float32)) * HGRN_HEAD_K ** -0.5
    q = q.reshape(B, L, HGRN_HEADS, HGRN_HEAD_K)
    i = p[..., 3 * kw:3 * kw + vw].astype(jnp.float32).reshape(B, L, HGRN_HEADS, HGRN_HEAD_V)
    g = p[..., 3 * kw + vw:].reshape(B, L, HGRN_HEADS, HGRN_HEAD_V)
    lb = lb.astype(jnp.float32)

    def direction(z, lb_d, s0, flip):
        f = lb_d + (1 - lb_d) * jax.nn.sigmoid(z.astype(jnp.float32))
        f = f.reshape(B, L, HGRN_HEADS, HGRN_HEAD_K)
        args = (q, 1 - f, i, jnp.log(f))
        if flip:
            args = tuple(jnp.flip(a, axis=1) for a in args)
        o, s = gla_scan(*args, s0.astype(jnp.float32))
        if flip:
            o = jnp.flip(o, axis=1)
        return o, s

    o_f, s_f = direction(p[..., kw:2 * kw], lb[0], s_init[:, 0], False)
    o_b, s_b = direction(p[..., 2 * kw:3 * kw], lb[1], s_init[:, 1], True)
    o = rmsnorm(o_f + o_b, norm_g) * jax.nn.silu(g.astype(jnp.float32))
    out = o.reshape(B, L, vw).astype(h.dtype) @ w_out
    return out, jnp.stack([s_f, s_b], axis=1)


def mla_down(h, w_down, q_norm_g, kv_norm_g):
    p = h @ w_down
    cq = rmsnorm(p[..., :MLA_Q_LORA], q_norm_g)
    ckv = rmsnorm(p[..., MLA_Q_LORA:MLA_Q_LORA + MLA_KV_LORA], kv_norm_g)
    kpe = p[..., MLA_Q_LORA + MLA_KV_LORA:]
    return cq, ckv, kpe


def mla_queries(cq, w_uq):
    B, L, _ = cq.shape
    return (cq @ w_uq).reshape(B, L, MLA_HEADS, MLA_QK)


def mla_keys_values(ckv, kpe, w_ukv):
    B, L, _ = ckv.shape
    kv = (ckv @ w_ukv).reshape(B, L, MLA_HEADS, MLA_NOPE + MLA_V)
    k_pe = jnp.broadcast_to(kpe[:, :, None, :], (B, L, MLA_HEADS, MLA_ROPE))
    return jnp.concatenate([kv[..., :MLA_NOPE], k_pe], axis=-1), kv[..., MLA_NOPE:]


def mla_context(h, w_down, q_norm_g, kv_norm_g, w_uq, w_ukv, w_out):
    B, L, _ = h.shape
    cq, ckv, kpe = mla_down(h, w_down, q_norm_g, kv_norm_g)
    q = mla_queries(cq, w_uq)
    k, v = mla_keys_values(ckv, kpe, w_ukv)
    o = dense_attention(q, k, v, MLA_QK ** -0.5, None)
    return o.reshape(B, L, MLA_HEADS * MLA_V) @ w_out, ckv, kpe


def mla_latent(h, ckv_ctx, kpe_ctx, row, col, w_down, q_norm_g, kv_norm_g, w_uq, w_ukv, w_out):
    B, L, _ = h.shape
    cq, ckv, kpe = mla_down(h, w_down, q_norm_g, kv_norm_g)
    q = mla_queries(cq, w_uq)
    q = jnp.concatenate([q[..., :MLA_NOPE], axial_rope(q[..., MLA_NOPE:], row, col)], axis=-1)
    kpe = axial_rope(kpe[:, :, None, :], row, col)[:, :, 0, :]
    k_lat, v_lat = mla_keys_values(ckv, kpe, w_ukv)
    k_ctx, v_ctx = mla_keys_values(ckv_ctx, kpe_ctx, w_ukv)
    k = jnp.concatenate([k_ctx, k_lat], axis=1)
    v = jnp.concatenate([v_ctx, v_lat], axis=1)
    o = dense_attention(q, k, v, MLA_QK ** -0.5, None)
    return o.reshape(B, L, MLA_HEADS * MLA_V) @ w_out


def swa_qkv(h, w_qkv):
    B, L, _ = h.shape
    p = h @ w_qkv
    q = p[..., :SWA_QW].reshape(B, L, SWA_Q_HEADS, SWA_HEAD_DIM)
    k = p[..., SWA_QW:SWA_QW + SWA_KVW].reshape(B, L, SWA_KV_HEADS, SWA_HEAD_DIM)
    v = p[..., SWA_QW + SWA_KVW:].reshape(B, L, SWA_KV_HEADS, SWA_HEAD_DIM)
    return q, k, v


def swa_context(h, w_qkv, sink, w_out):
    B, L, _ = h.shape
    q, k, v = swa_qkv(h, w_qkv)
    o = dense_attention(q, k, v, SWA_HEAD_DIM ** -0.5, sink)
    return o.reshape(B, L, SWA_QW) @ w_out, k, v


def swa_latent(h, k_ctx, v_ctx, row, col, w_qkv, sink, w_out):
    B, L, _ = h.shape
    q, k, v = swa_qkv(h, w_qkv)
    q = axial_rope(q, row, col)
    k = axial_rope(k, row, col)
    o = window_attention(q, k, v, k_ctx, v_ctx, sink, SWA_HEAD_DIM ** -0.5)
    return o.reshape(B, L, SWA_QW) @ w_out


def sq_relu_mlp(h, w_in, w_out):
    return jnp.square(jax.nn.relu(h @ w_in)) @ w_out


def setup_inputs(seed: int = 0) -> dict:
    key = jax.random.key(seed)
    ks = iter(jax.random.split(key, 32))

    def nrm(shape, scale):
        return jax.random.normal(next(ks), shape, jnp.float32) * scale

    def gain(shape):
        return 1.0 + nrm(shape, 0.02)

    return {
        'x_prompt': nrm((BATCH, SEQ, D_MODEL), 1.0),
        'x_sample': nrm((DEC_BATCH, DEC_SEQ, D_MODEL), 1.0),
        'c': nrm((DEC_BATCH, D_MODEL), 1.0),
        'state_hgrn': nrm((DEC_BATCH, N_A_LAYERS, 2, HGRN_HEADS, HGRN_HEAD_K, HGRN_HEAD_V), 0.5),
        'cache_mla_ckv': nrm((DEC_BATCH, N_B_LAYERS, PAST_LEN, MLA_KV_LORA), 1.0),
        'cache_mla_kpe': nrm((DEC_BATCH, N_B_LAYERS, PAST_LEN, MLA_ROPE), 1.0),
        'cache_swa_k': nrm((DEC_BATCH, N_C_LAYERS, PAST_LEN, SWA_KV_HEADS, SWA_HEAD_DIM), 1.0),
        'cache_swa_v': nrm((DEC_BATCH, N_C_LAYERS, PAST_LEN, SWA_KV_HEADS, SWA_HEAD_DIM), 1.0),
        'c_ctx': nrm((D_MODEL,), 1.0),
        'ada_w': nrm((DEPTH, D_MODEL, N_MOD * D_MODEL), 0.5 * D_MODEL ** -0.5),
        'ada_b': nrm((DEPTH, N_MOD * D_MODEL), 0.02),
        'norm_g': gain((DEPTH, 4, D_MODEL)),
        'mlp_w_in': nrm((DEPTH, D_MODEL, D_FF), D_MODEL ** -0.5),
        'mlp_w_out': nrm((DEPTH, D_FF, D_MODEL), D_FF ** -0.5),
        'hgrn_w_in': nrm((N_A_LAYERS, D_MODEL, HGRN_IN), D_MODEL ** -0.5),
        'hgrn_lb_logits': nrm((2, DEPTH, HGRN_KW), 0.5),
        'hgrn_norm_g': gain((N_A_LAYERS, HGRN_HEAD_V)),
        'hgrn_w_out': nrm((N_A_LAYERS, HGRN_VW, D_MODEL), HGRN_VW ** -0.5),
        'mla_w_down': nrm((N_B_LAYERS, D_MODEL, MLA_DOWN), D_MODEL ** -0.5),
        'mla_q_norm_g': gain((N_B_LAYERS, MLA_Q_LORA)),
        'mla_kv_norm_g': gain((N_B_LAYERS, MLA_KV_LORA)),
        'mla_w_uq': nrm((N_B_LAYERS, MLA_Q_LORA, MLA_HEADS * MLA_QK), MLA_Q_LORA ** -0.5),
        'mla_w_ukv': nrm((N_B_LAYERS, MLA_KV_LORA, MLA_HEADS * (MLA_NOPE + MLA_V)), MLA_KV_LORA ** -0.5),
        'mla_w_out': nrm((N_B_LAYERS, MLA_HEADS * MLA_V, D_MODEL), (MLA_HEADS * MLA_V) ** -0.5),
        'swa_w_qkv': nrm((N_C_LAYERS, D_MODEL, SWA_QW + 2 * SWA_KVW), D_MODEL ** -0.5),
        'swa_sink': nrm((N_C_LAYERS, SWA_Q_HEADS), 0.5),
        'swa_w_out': nrm((N_C_LAYERS, SWA_QW, D_MODEL), SWA_QW ** -0.5),
    }


def reference(x_prompt, x_sample, c, state_hgrn, cache_mla_ckv, cache_mla_kpe, cache_swa_k, cache_swa_v,
              c_ctx, ada_w, ada_b, norm_g, mlp_w_in, mlp_w_out,
              hgrn_w_in, hgrn_lb_logits, hgrn_norm_g, hgrn_w_out,
              mla_w_down, mla_q_norm_g, mla_kv_norm_g, mla_w_uq, mla_w_ukv, mla_w_out,
              swa_w_qkv, swa_sink, swa_w_out):
    row, col = grid_positions(x_sample.shape[1])
    lb_soft = jax.nn.softmax(hgrn_lb_logits.astype(jnp.float32), axis=1)
    lb_all = jnp.cumsum(lb_soft, axis=1) - lb_soft[:, :1]

    yp, ys = x_prompt, x_sample
    new_hgrn, new_ckv, new_kpe, new_k, new_v = [], [], [], [], []
    for layer in range(DEPTH):
        kind, j = layer % N_MIXERS, layer // N_MIXERS
        mp = adaln_params(c_ctx[None, :], ada_w[layer], ada_b[layer])
        ms = adaln_params(c, ada_w[layer], ada_b[layer])
        g = norm_g[layer]

        hp = pre_sublayer(yp, g[0], mp, 0)
        hs = pre_sublayer(ys, g[0], ms, 0)
        if kind == 0:
            zero_state = jnp.zeros((yp.shape[0], 2, HGRN_HEADS, HGRN_HEAD_K, HGRN_HEAD_V), jnp.float32)
            op, st = hgrn_mix(hp, zero_state, hgrn_w_in[j], lb_all[:, layer], hgrn_norm_g[j], hgrn_w_out[j])
            os, _ = hgrn_mix(hs, state_hgrn[:, j], hgrn_w_in[j], lb_all[:, layer], hgrn_norm_g[j], hgrn_w_out[j])
            new_hgrn.append(st.astype(state_hgrn.dtype))
        elif kind == 1:
            op, ckv, kpe = mla_context(hp, mla_w_down[j], mla_q_norm_g[j], mla_kv_norm_g[j],
                                       mla_w_uq[j], mla_w_ukv[j], mla_w_out[j])
            os = mla_latent(hs, cache_mla_ckv[:, j], cache_mla_kpe[:, j], row, col, mla_w_down[j],
                            mla_q_norm_g[j], mla_kv_norm_g[j], mla_w_uq[j], mla_w_ukv[j], mla_w_out[j])
            new_ckv.append(ckv)
            new_kpe.append(kpe)
        else:
            op, kc, vc = swa_context(hp, swa_w_qkv[j], swa_sink[j], swa_w_out[j])
            os = swa_latent(hs, cache_swa_k[:, j], cache_swa_v[:, j], row, col,
                            swa_w_qkv[j], swa_sink[j], swa_w_out[j])
            new_k.append(kc)
            new_v.append(vc)
        yp = post_sublayer(yp, op, g[1], mp, 0)
        ys = post_sublayer(ys, os, g[1], ms, 0)

        yp = post_sublayer(yp, sq_relu_mlp(pre_sublayer(yp, g[2], mp, 3), mlp_w_in[layer], mlp_w_out[layer]), g[3], mp, 3)
        ys = post_sublayer(ys, sq_relu_mlp(pre_sublayer(ys, g[2], ms, 3), mlp_w_in[layer], mlp_w_out[layer]), g[3], ms, 3)

    return (yp, ys, jnp.stack(new_hgrn, axis=1), jnp.stack(new_ckv, axis=1), jnp.stack(new_kpe, axis=1),
            jnp.stack(new_k, axis=1), jnp.stack(new_v, axis=1))
```

```python
import functools

import jax
import jax.numpy as jnp
from jax import lax
from jax.experimental import pallas as pl
from jax.experimental.pallas import tpu as pltpu

F32 = jnp.float32
BF16 = jnp.bfloat16

D_MODEL = 2048
DEPTH = 4
N_PROMPT_SEQ = 16
PROMPT_LEN = 256
N_SAMPLE_SEQ = 4
SAMPLE_LEN = 4096
PAST_LEN = 256
NP_TOK = N_PROMPT_SEQ * PROMPT_LEN
NS_TOK = N_SAMPLE_SEQ * SAMPLE_LEN
NT_TOK = NP_TOK + NS_TOK
GROUP_TOK = 4096
N_GROUPS = NT_TOK // GROUP_TOK
N_MOD = 6
D_FF = 4 * D_MODEL
NORM_EPS = 1e-6
GRID_W = 64
ROPE_BASE = 10000.0

HGRN_HEADS = 16
HGRN_HD = 128
HGRN_CHUNK = 128
HGRN_BLOCK = 256
HGRN_DIAG = 8

MLA_HEADS = 16
MLA_Q_LORA = 512
MLA_KV_LORA = 512
MLA_NOPE = 128
MLA_ROPE = 64
MLA_V = 128
MLA_QK = MLA_NOPE + MLA_ROPE
MLA_QK_PAD = 256
MLA_DOWN_PAD = 1152
MLA_KV_LEN = PAST_LEN + SAMPLE_LEN

SWA_HD = 64
SWA_Q_HEADS = 32
SWA_KV_HEADS = 8
SWA_GROUP = SWA_Q_HEADS // SWA_KV_HEADS
SWA_QW = SWA_Q_HEADS * SWA_HD
SWA_KVW = SWA_KV_HEADS * SWA_HD
SWA_BLOCK = 128
NEG_INF = -1e30

TM = 512
TILES_PER_GROUP = GROUP_TOK // TM
VMEM_LIMIT = 56 * 1024 * 1024


def _cparams(sem):
    return pltpu.CompilerParams(dimension_semantics=sem, vmem_limit_bytes=VMEM_LIMIT)


def _dot(a, b):
    return jnp.dot(a, b, preferred_element_type=F32)


def _dot_nt(a, b):
    return lax.dot_general(a, b, (((1,), (1,)), ((), ())), preferred_element_type=F32)


def _rms(x, g):
    return x * lax.rsqrt(jnp.mean(x * x, axis=-1, keepdims=True) + NORM_EPS) * g


def _silu(x):
    return x * jax.nn.sigmoid(x)


def _prenorm(y_ref, g_ref, sh_ref, sc_ref):
    return (_rms(y_ref[...], g_ref[0]) * (1.0 + sc_ref[0, 0]) + sh_ref[0, 0]).astype(BF16)


def _post(y_ref, acc, g_ref, gate_ref):
    return y_ref[...] + gate_ref[0, 0] * _rms(acc, g_ref[0])


def _rope128(x, cos, sin):
    lane = lax.broadcasted_iota(jnp.int32, x.shape, 1)
    first = (lane & 31) < 16
    swapped = jnp.where(first, pltpu.roll(x, 112, 1), pltpu.roll(x, 16, 1))
    return x * cos + swapped * sin


def _tok_spec(n):
    return pl.BlockSpec((TM, n), lambda i, *_: (i, 0))


def _norm_spec(k):
    return pl.BlockSpec((1, 1, D_MODEL), lambda i, *_: (k, 0, 0))


def _mod_spec(k):
    return pl.BlockSpec((1, 1, 1, D_MODEL), lambda i, *_: (i // TILES_PER_GROUP, k, 0, 0))


def _rope_spec():
    def idx(i, *_):
        return (jnp.where(i < TILES_PER_GROUP, i, TILES_PER_GROUP + i % TILES_PER_GROUP), 0)
    return pl.BlockSpec((TM, 128), idx)


def _resident(shape):
    nd = len(shape)
    return pl.BlockSpec(shape, lambda *_: (0,) * nd, pipeline_mode=pl.Buffered(1))


def _adaln_kernel(c_ref, w_ref, b_ref, o_ref):
    s = _silu(c_ref[...])
    s_hi = s.astype(BF16)
    s_lo = (s - s_hi.astype(F32)).astype(BF16)
    w = w_ref[0]
    w_hi = w.astype(BF16)
    w_lo = (w - w_hi.astype(F32)).astype(BF16)
    o_ref[0] = _dot(s_hi, w_hi) + _dot(s_lo, w_hi) + _dot(s_hi, w_lo) + b_ref[0]


def _adaln(cond8, ada_w, ada_b):
    tn = 1024
    n = N_MOD * D_MODEL
    return pl.pallas_call(
        _adaln_kernel,
        grid=(DEPTH, n // tn),
        in_specs=[
            pl.BlockSpec((8, D_MODEL), lambda l, j: (0, 0)),
            pl.BlockSpec((1, D_MODEL, tn), lambda l, j: (l, 0, j)),
            pl.BlockSpec((1, 1, tn), lambda l, j: (l, 0, j)),
        ],
        out_specs=pl.BlockSpec((1, 8, tn), lambda l, j: (l, 0, j)),
        out_shape=jax.ShapeDtypeStruct((DEPTH, 8, n), F32),
        compiler_params=_cparams(("parallel", "parallel")),
        name="adaln",
    )(cond8, ada_w, ada_b.reshape(DEPTH, 1, n))


def _mlp_kernel(y_ref, g2_ref, g3_ref, sh_ref, sc_ref, gate_ref, w1_ref, w2_ref, o_ref, h_ref):
    f = pl.program_id(1)

    @pl.when(f == 0)
    def _():
        h_ref[...] = _prenorm(y_ref, g2_ref, sh_ref, sc_ref)
        o_ref[...] = jnp.zeros_like(o_ref)

    u = jnp.maximum(_dot(h_ref[...], w1_ref[...]), 0.0)
    o_ref[...] += _dot((u * u).astype(BF16), w2_ref[...])

    @pl.when(f == pl.num_programs(1) - 1)
    def _():
        o_ref[...] = _post(y_ref, o_ref[...], g3_ref, gate_ref)


def _mlp(y, norm_l, mods_l, w1, w2):
    tf = 1024
    return pl.pallas_call(
        _mlp_kernel,
        grid=(NT_TOK // TM, D_FF // tf),
        in_specs=[
            _tok_spec(D_MODEL), _norm_spec(2), _norm_spec(3),
            _mod_spec(3), _mod_spec(4), _mod_spec(5),
            pl.BlockSpec((D_MODEL, tf), lambda i, f: (0, f)),
            pl.BlockSpec((tf, D_MODEL), lambda i, f: (f, 0)),
        ],
        out_specs=_tok_spec(D_MODEL),
        out_shape=jax.ShapeDtypeStruct((NT_TOK, D_MODEL), F32),
        scratch_shapes=[pltpu.VMEM((TM, D_MODEL), BF16)],
        compiler_params=_cparams(("parallel", "arbitrary")),
        name="mlp",
    )(y, norm_l, norm_l, mods_l, mods_l, mods_l, w1, w2)


def _outproj_kernel(x_ref, y_ref, g1_ref, gate_ref, w_ref, o_ref):
    o_ref[...] = _post(y_ref, _dot(x_ref[...], w_ref[...]), g1_ref, gate_ref)


def _outproj(x, y, norm_l, mods_l, w):
    return pl.pallas_call(
        _outproj_kernel,
        grid=(NT_TOK // TM,),
        in_specs=[_tok_spec(D_MODEL), _tok_spec(D_MODEL), _norm_spec(1), _mod_spec(2),
                  _resident((D_MODEL, D_MODEL))],
        out_specs=_tok_spec(D_MODEL),
        out_shape=jax.ShapeDtypeStruct((NT_TOK, D_MODEL), F32),
        compiler_params=_cparams(("parallel",)),
        name="outproj",
    )(x, y, norm_l, mods_l, w)


def _hgrn_in_kernel(y_ref, g0_ref, sh_ref, sc_ref, w_ref, o_ref, h_ref):
    @pl.when(pl.program_id(1) == 0)
    def _():
        h_ref[...] = _prenorm(y_ref, g0_ref, sh_ref, sc_ref)

    acc = _dot(h_ref[...], w_ref[...])
    for c in range(HGRN_HEADS):
        o_ref[0, c] = acc[:, c * HGRN_HD:(c + 1) * HGRN_HD].astype(BF16)


def _hgrn_in(y, norm_l, mods_l, w):
    return pl.pallas_call(
        _hgrn_in_kernel,
        grid=(NT_TOK // TM, 5),
        in_specs=[_tok_spec(D_MODEL), _norm_spec(0), _mod_spec(0), _mod_spec(1),
                  pl.BlockSpec((D_MODEL, D_MODEL), lambda i, j: (0, j))],
        out_specs=pl.BlockSpec((1, HGRN_HEADS, TM, HGRN_HD), lambda i, j: (j, 0, i, 0)),
        out_shape=jax.ShapeDtypeStruct((5, HGRN_HEADS, NT_TOK, HGRN_HD), BF16),
        scratch_shapes=[pltpu.VMEM((TM, D_MODEL), BF16)],
        compiler_params=_cparams(("parallel", "arbitrary")),
        name="hgrn_in",
    )(y, norm_l, mods_l, mods_l, w)


HGRN_NBLK = NT_TOK // HGRN_BLOCK
HGRN_PBLK = NP_TOK // HGRN_BLOCK
HGRN_SBLK = SAMPLE_LEN // HGRN_BLOCK
HGRN_NSEQ = N_PROMPT_SEQ + N_SAMPLE_SEQ


def _hgrn_seq_of_block(blk):
    return jnp.where(blk < HGRN_PBLK, blk, HGRN_PBLK + (blk - HGRN_PBLK) // HGRN_SBLK)


def _hgrn_scan_kernel(zq_ref, zf_ref, zv_ref, lb_ref, s0_ref, o_ref, sfin_ref, st_ref, *, rev):
    c = HGRN_CHUNK
    nsub = HGRN_BLOCK // c
    i = pl.program_id(0)
    blk = (HGRN_NBLK - 1 - i) if rev else i
    is_prompt = blk < HGRN_PBLK
    pos = (blk - HGRN_PBLK) % HGRN_SBLK
    seq_start = jnp.logical_or(is_prompt, pos == (HGRN_SBLK - 1 if rev else 0))
    seq_end = jnp.logical_or(is_prompt, pos == (0 if rev else HGRN_SBLK - 1))

    @pl.when(jnp.logical_and(seq_start, is_prompt))
    def _():
        st_ref[...] = jnp.zeros_like(st_ref)

    @pl.when(jnp.logical_and(seq_start, jnp.logical_not(is_prompt)))
    def _():
        def init(h, carry):
            st_ref[h] = s0_ref[0, h].T
            return carry
        lax.fori_loop(0, HGRN_HEADS, init, 0)

    row = lax.broadcasted_iota(jnp.int32, (c, HGRN_HD), 0)
    r8 = row & (HGRN_DIAG - 1)
    tt = lax.broadcasted_iota(jnp.int32, (c, c), 0)
    ss = lax.broadcasted_iota(jnp.int32, (c, c), 1)
    tri = ((ss >= tt) if rev else (ss <= tt)).astype(BF16)
    levels = []
    size = 2 * HGRN_DIAG
    while size <= c:
        in_hi = (row & (size - 1)) >= size // 2
        is_query = jnp.logical_not(in_hi) if rev else in_hi
        sh = size.bit_length() - 1
        same = ((tt >> sh) == (ss >> sh)).astype(F32) if size < c else None
        levels.append((size, is_query, same))
        size *= 2

    def shift(x, n):
        if n == 0:
            return x
        return pltpu.roll(x, (c - n) if rev else n, 0)

    def chunk(step, carry):
        h = step // nsub
        ci = step % nsub
        ci = (nsub - 1 - ci) if rev else ci
        r0 = pl.multiple_of(ci * c, c)
        zq = zq_ref[0, h, pl.ds(r0, c), :].astype(F32)
        zf = zf_ref[0, h, pl.ds(r0, c), :].astype(F32)
        v = zv_ref[0, h, pl.ds(r0, c), :].astype(F32)
        lb = lb_ref[h]
        q = _silu(zq) * (HGRN_HD ** -0.5)
        f = lb + (1.0 - lb) * jax.nn.sigmoid(zf)
        k = 1.0 - f
        g = jnp.log(f)
        g1 = g.astype(BF16)
        r1 = g - g1.astype(F32)
        g2 = r1.astype(BF16)
        g3 = (r1 - g2.astype(F32)).astype(BF16)
        cum = _dot(tri, g1) + _dot(tri, g2) + _dot(tri, g3)
        tot = cum[0:1, :] if rev else cum[c - 1:c, :]
        st = st_ref[h]
        vb = v.astype(BF16)

        o = _dot_nt((q * jnp.exp(cum)).astype(BF16), st.astype(BF16))

        o = o + jnp.sum(q * k, axis=-1, keepdims=True) * v
        decay = jnp.ones_like(f)
        for d in range(1, HGRN_DIAG):
            decay = decay * shift(f, d - 1)
            keep = (r8 <= HGRN_DIAG - 1 - d) if rev else (r8 >= d)
            pair = jnp.where(keep, q * shift(k, d) * decay, 0.0)
            o = o + jnp.sum(pair, axis=-1, keepdims=True) * shift(v, d)

        att = jnp.zeros((c, c), F32)
        for size, is_query, same in levels:
            half = size // 2
            pieces = []
            for b in range(c // size):
                ref_row = b * size + (half if rev else half - 1)
                pieces.append(jnp.broadcast_to(cum[ref_row:ref_row + 1, :], (size, HGRN_HD)))
            ref = pieces[0] if len(pieces) == 1 else jnp.concatenate(pieces, axis=0)
            dlt = cum - ref
            x = jnp.where(is_query, q, k) * jnp.exp(jnp.where(is_query, dlt, -dlt))
            xq = jnp.where(is_query, x, 0.0).astype(BF16)
            xk = jnp.where(is_query, 0.0, x).astype(BF16)
            a = _dot_nt(xq, xk)
            att = att + (a if same is None else a * same)
        o = o + _dot(att.astype(BF16), vb)
        o_ref[h, pl.ds(r0, c), :] = o.astype(BF16)

        ke = (k * jnp.exp(tot - cum)).astype(BF16)
        st_ref[h] = jnp.exp(tot) * st + _dot(v.T.astype(BF16), ke)
        return carry

    lax.fori_loop(0, HGRN_HEADS * nsub, chunk, 0)

    @pl.when(seq_end)
    def _():
        def fin(h, carry):
            sfin_ref[0, h] = st_ref[h].T
            return carry
        lax.fori_loop(0, HGRN_HEADS, fin, 0)


def _hgrn_scan(p_hm, lb_d, s0_d, rev):
    def blk_of(i):
        return (HGRN_NBLK - 1 - i) if rev else i

    def sec_spec(sec):
        return pl.BlockSpec((1, HGRN_HEADS, HGRN_BLOCK, HGRN_HD), lambda i: (sec, 0, blk_of(i), 0))

    def seq_idx(i):
        return _hgrn_seq_of_block(blk_of(i))

    return pl.pallas_call(
        functools.partial(_hgrn_scan_kernel, rev=rev),
        grid=(HGRN_NBLK,),
        in_specs=[
            sec_spec(0), sec_spec(2 if rev else 1), sec_spec(3),
            pl.BlockSpec((HGRN_HEADS, 1, HGRN_HD), lambda i: (0, 0, 0)),
            pl.BlockSpec((1, HGRN_HEADS, HGRN_HD, HGRN_HD),
                         lambda i: (jnp.maximum(seq_idx(i) - N_PROMPT_SEQ, 0), 0, 0, 0)),
        ],
        out_specs=[
            pl.BlockSpec((HGRN_HEADS, HGRN_BLOCK, HGRN_HD), lambda i: (0, blk_of(i), 0)),
            pl.BlockSpec((1, HGRN_HEADS, HGRN_HD, HGRN_HD), lambda i: (seq_idx(i), 0, 0, 0)),
        ],
        out_shape=[
            jax.ShapeDtypeStruct((HGRN_HEADS, NT_TOK, HGRN_HD), BF16),
            jax.ShapeDtypeStruct((HGRN_NSEQ, HGRN_HEADS, HGRN_HD, HGRN_HD), F32),
        ],
        scratch_shapes=[pltpu.VMEM((HGRN_HEADS, HGRN_HD, HGRN_HD), F32)],
        compiler_params=_cparams(("arbitrary",)),
        name="hgrn_scan_bwd" if rev else "hgrn_scan_fwd",
    )(p_hm, p_hm, p_hm, lb_d, s0_d)


def _hgrn_out_kernel(of_ref, ob_ref, zg_ref, ng_ref, y_ref, g1_ref, gate_ref, w_ref, o_ref, x_ref):
    ng = ng_ref[...]
    for h in range(HGRN_HEADS):
        o = of_ref[h].astype(F32) + ob_ref[h].astype(F32)
        x_ref[:, h * HGRN_HD:(h + 1) * HGRN_HD] = (_rms(o, ng) * _silu(zg_ref[0, h].astype(F32))).astype(BF16)
    o_ref[...] = _post(y_ref, _dot(x_ref[...], w_ref[...]), g1_ref, gate_ref)


def _hgrn_out(o_f, o_b, p_hm, hgrn_norm_g, y, norm_l, mods_l, w):
    head_spec = pl.BlockSpec((HGRN_HEADS, TM, HGRN_HD), lambda i: (0, i, 0))
    return pl.pallas_call(
        _hgrn_out_kernel,
        grid=(NT_TOK // TM,),
        in_specs=[
            head_spec, head_spec,
            pl.BlockSpec((1, HGRN_HEADS, TM, HGRN_HD), lambda i: (4, 0, i, 0)),
            pl.BlockSpec((1, HGRN_HD), lambda i: (0, 0)),
            _tok_spec(D_MODEL), _norm_spec(1), _mod_spec(2),
            _resident((D_MODEL, D_MODEL)),
        ],
        out_specs=_tok_spec(D_MODEL),
        out_shape=jax.ShapeDtypeStruct((NT_TOK, D_MODEL), F32),
        scratch_shapes=[pltpu.VMEM((TM, D_MODEL), BF16)],
        compiler_params=_cparams(("parallel",)),
        name="hgrn_out",
    )(o_f, o_b, p_hm, hgrn_norm_g, y, norm_l, mods_l, w)


def _mla_down_kernel(y_ref, g0_ref, sh_ref, sc_ref, w_ref, qg_ref, kvg_ref, cos_ref, sin_ref,
                     cq_ref, ckv_ref, kpe_ref, kpe128_ref):
    p = _dot(_prenorm(y_ref, g0_ref, sh_ref, sc_ref), w_ref[...])
    cq_ref[...] = _rms(p[:, :MLA_Q_LORA], qg_ref[...]).astype(BF16)
    ckv_ref[...] = _rms(p[:, MLA_Q_LORA:MLA_Q_LORA + MLA_KV_LORA], kvg_ref[...])
    tail = p[:, MLA_Q_LORA + MLA_KV_LORA:]
    kpe_ref[...] = tail[:, :MLA_ROPE]
    kpe128_ref[...] = _rope128(tail, cos_ref[...], sin_ref[...]).astype(BF16)


def _mla_down(y, norm_l, mods_l, w, qg, kvg, cos_t, sin_t):
    return pl.pallas_call(
        _mla_down_kernel,
        grid=(NT_TOK // TM,),
        in_specs=[_tok_spec(D_MODEL), _norm_spec(0), _mod_spec(0), _mod_spec(1),
                  _resident((D_MODEL, MLA_DOWN_PAD)),
                  pl.BlockSpec((1, MLA_Q_LORA), lambda i: (0, 0)),
                  pl.BlockSpec((1, MLA_KV_LORA), lambda i: (0, 0)),
                  _rope_spec(), _rope_spec()],
        out_specs=[_tok_spec(MLA_Q_LORA), _tok_spec(MLA_KV_LORA), _tok_spec(MLA_ROPE), _tok_spec(128)],
        out_shape=[
            jax.ShapeDtypeStruct((NT_TOK, MLA_Q_LORA), BF16),
            jax.ShapeDtypeStruct((NT_TOK, MLA_KV_LORA), F32),
            jax.ShapeDtypeStruct((NT_TOK, MLA_ROPE), F32),
            jax.ShapeDtypeStruct((NT_TOK, 128), BF16),
        ],
        compiler_params=_cparams(("parallel",)),
        name="mla_down",
    )(y, norm_l, mods_l, mods_l, w, qg, kvg, cos_t, sin_t)


def _mla_q_kernel(cq_ref, w_ref, cos_ref, sin_ref, q_ref):
    acc = _dot(cq_ref[...], w_ref[0])
    q_ref[0, :, :MLA_NOPE] = acc[:, :MLA_NOPE].astype(BF16)
    q_ref[0, :, MLA_NOPE:] = _rope128(acc[:, MLA_NOPE:], cos_ref[...], sin_ref[...]).astype(BF16)


def _mla_q(cq, w_uq_h, cos_t, sin_t):
    return pl.pallas_call(
        _mla_q_kernel,
        grid=(NT_TOK // TM, MLA_HEADS),
        in_specs=[_tok_spec(MLA_Q_LORA),
                  pl.BlockSpec((1, MLA_Q_LORA, MLA_QK_PAD), lambda i, h: (h, 0, 0)),
                  _rope_spec(), _rope_spec()],
        out_specs=pl.BlockSpec((1, TM, MLA_QK_PAD), lambda i, h: (h, i, 0)),
        out_shape=jax.ShapeDtypeStruct((MLA_HEADS, NT_TOK, MLA_QK_PAD), BF16),
        compiler_params=_cparams(("parallel", "arbitrary")),
        name="mla_q",
    )(cq, w_uq_h, cos_t, sin_t)


def _mla_kv_kernel(ckv_ref, kpe_ref, w_ref, k_ref, v_ref):
    acc = _dot(ckv_ref[...], w_ref[0])
    k_ref[0, :, :MLA_NOPE] = acc[:, :MLA_NOPE].astype(BF16)
    k_ref[0, :, MLA_NOPE:] = kpe_ref[...]
    v_ref[0] = acc[:, MLA_NOPE:].astype(BF16)


def _mla_kv(ckv_all, kpe_all, w_ukv_h):
    n = ckv_all.shape[0]
    return pl.pallas_call(
        _mla_kv_kernel,
        grid=(n // TM, MLA_HEADS),
        in_specs=[_tok_spec(MLA_KV_LORA), _tok_spec(128),
                  pl.BlockSpec((1, MLA_KV_LORA, MLA_NOPE + MLA_V), lambda i, h: (h, 0, 0))],
        out_specs=[pl.BlockSpec((1, TM, MLA_QK_PAD), lambda i, h: (h, i, 0)),
                   pl.BlockSpec((1, TM, MLA_V), lambda i, h: (h, i, 0))],
        out_shape=[jax.ShapeDtypeStruct((MLA_HEADS, n, MLA_QK_PAD), BF16),
                   jax.ShapeDtypeStruct((MLA_HEADS, n, MLA_V), BF16)],
        compiler_params=_cparams(("parallel", "arbitrary")),
        name="mla_kv",
    )(ckv_all, kpe_all, w_ukv_h)


def _mla_attn_kernel(q_ref, k_ref, v_ref, *rest):
    o_ref = rest[-1]
    s = _dot_nt(q_ref[0], k_ref[0]) * (MLA_QK ** -0.5)
    m = jnp.max(s, axis=-1, keepdims=True)
    p = jnp.exp(s - m)
    den = jnp.sum(p, axis=-1, keepdims=True)
    o_ref[...] = (_dot(p.astype(BF16), v_ref[0]) / den).astype(BF16)


MLA_TQ = 256


def _mla_attn(q_hm, k_hm, v_hm):
    tq = MLA_TQ
    out_shape = jax.ShapeDtypeStruct((NT_TOK, MLA_HEADS * MLA_V), BF16)
    pk0 = N_SAMPLE_SEQ * MLA_KV_LEN // PROMPT_LEN
    o = pl.pallas_call(
        _mla_attn_kernel,
        grid=(MLA_HEADS, N_PROMPT_SEQ),
        in_specs=[pl.BlockSpec((1, PROMPT_LEN, MLA_QK_PAD), lambda h, b: (h, b, 0)),
                  pl.BlockSpec((1, PROMPT_LEN, MLA_QK_PAD), lambda h, b: (h, pk0 + b, 0)),
                  pl.BlockSpec((1, PROMPT_LEN, MLA_V), lambda h, b: (h, pk0 + b, 0))],
        out_specs=pl.BlockSpec((PROMPT_LEN, MLA_V), lambda h, b: (b, h)),
        out_shape=out_shape,
        compiler_params=_cparams(("parallel", "parallel")),
        name="mla_attn_prompt",
    )(q_hm, k_hm, v_hm)
    q0 = NP_TOK // tq
    nq = SAMPLE_LEN // tq
    return pl.pallas_call(
        _mla_attn_kernel,
        grid=(MLA_HEADS, N_SAMPLE_SEQ, nq),
        in_specs=[pl.BlockSpec((1, tq, MLA_QK_PAD), lambda h, b, i: (h, q0 + b * nq + i, 0)),
                  pl.BlockSpec((1, MLA_KV_LEN, MLA_QK_PAD), lambda h, b, i: (h, b, 0)),
                  pl.BlockSpec((1, MLA_KV_LEN, MLA_V), lambda h, b, i: (h, b, 0)),
                  pl.BlockSpec(memory_space=pl.ANY)],
        out_specs=pl.BlockSpec((tq, MLA_V), lambda h, b, i: (q0 + b * nq + i, h)),
        out_shape=out_shape,
        input_output_aliases={3: 0},
        compiler_params=_cparams(("parallel", "parallel", "arbitrary")),
        name="mla_attn_sample",
    )(q_hm, k_hm, v_hm, o)


def _swa_qkv_kernel(y_ref, g0_ref, sh_ref, sc_ref, w_ref, cos_ref, sin_ref, q_ref, k_ref, v_ref, kf_ref, vf_ref):
    p = _dot(_prenorm(y_ref, g0_ref, sh_ref, sc_ref), w_ref[...])
    cos = cos_ref[...]
    sin = sin_ref[...]
    vf_ref[...] = p[:, SWA_QW + SWA_KVW:]
    kf_ref[...] = p[:, SWA_QW:SWA_QW + SWA_KVW]
    for c in range((SWA_QW + SWA_KVW) // 128):
        r = _rope128(p[:, c * 128:(c + 1) * 128], cos, sin).astype(BF16)
        for half in range(2):
            head = 2 * c + half
            piece = r[:, half * SWA_HD:(half + 1) * SWA_HD]
            if head < SWA_Q_HEADS:
                q_ref[head] = piece
            else:
                k_ref[head - SWA_Q_HEADS] = piece
    for head in range(SWA_KV_HEADS):
        v_ref[head] = p[:, SWA_QW + SWA_KVW + head * SWA_HD:SWA_QW + SWA_KVW + (head + 1) * SWA_HD].astype(BF16)


def _swa_qkv(y, norm_l, mods_l, w, cos_t, sin_t):
    def hm(nh):
        return pl.BlockSpec((nh, TM, SWA_HD), lambda i: (0, i, 0))
    return pl.pallas_call(
        _swa_qkv_kernel,
        grid=(NT_TOK // TM,),
        in_specs=[_tok_spec(D_MODEL), _norm_spec(0), _mod_spec(0), _mod_spec(1),
                  _resident((D_MODEL, SWA_QW + 2 * SWA_KVW)), _rope_spec(), _rope_spec()],
        out_specs=[hm(SWA_Q_HEADS), hm(SWA_KV_HEADS), hm(SWA_KV_HEADS),
                   _tok_spec(SWA_KVW), _tok_spec(SWA_KVW)],
        out_shape=[
            jax.ShapeDtypeStruct((SWA_Q_HEADS, NT_TOK, SWA_HD), BF16),
            jax.ShapeDtypeStruct((SWA_KV_HEADS, NT_TOK, SWA_HD), BF16),
            jax.ShapeDtypeStruct((SWA_KV_HEADS, NT_TOK, SWA_HD), BF16),
            jax.ShapeDtypeStruct((NT_TOK, SWA_KVW), F32),
            jax.ShapeDtypeStruct((NT_TOK, SWA_KVW), F32),
        ],
        compiler_params=_cparams(("parallel",)),
        name="swa_qkv",
    )(y, norm_l, mods_l, mods_l, w, cos_t, sin_t)


def _swa_attn_kernel(sink_ref, q_ref, kc_ref, vc_ref, *rest, local):
    o_ref = rest[-1]
    kv = pl.program_id(0)
    scale = SWA_HD ** -0.5
    kc = kc_ref[...].reshape(kc_ref.shape[-2], SWA_HD)
    vc = vc_ref[...].reshape(vc_ref.shape[-2], SWA_HD)
    if local:
        kl = [r[0] for r in rest[0:3]]
        vl = [r[0] for r in rest[3:6]]
        j = pl.program_id(2)
        nb = pl.num_programs(2)
        rr = lax.broadcasted_iota(jnp.int32, (SWA_BLOCK, SWA_BLOCK), 0)
        cc = lax.broadcasted_iota(jnp.int32, (SWA_BLOCK, SWA_BLOCK), 1)
        keep = [jnp.logical_and(cc >= rr, j > 0), None, jnp.logical_and(cc <= rr, j < nb - 1)]
    for g in range(SWA_GROUP):
        q = q_ref[g]
        sink = sink_ref[kv * SWA_GROUP + g]
        s_c = _dot_nt(q, kc) * scale
        m = jnp.maximum(jnp.max(s_c, axis=-1, keepdims=True), sink)
        s_l = []
        if local:
            for t in range(3):
                s = _dot_nt(q, kl[t]) * scale
                if keep[t] is not None:
                    s = jnp.where(keep[t], s, NEG_INF)
                s_l.append(s)
                m = jnp.maximum(m, jnp.max(s, axis=-1, keepdims=True))
        p_c = jnp.exp(s_c - m)
        den = jnp.sum(p_c, axis=-1, keepdims=True) + jnp.exp(sink - m)
        acc = _dot(p_c.astype(BF16), vc)
        for t in range(len(s_l)):
            p = jnp.exp(s_l[t] - m)
            den = den + jnp.sum(p, axis=-1, keepdims=True)
            acc = acc + _dot(p.astype(BF16), vl[t])
        o_ref[:, g * SWA_HD:(g + 1) * SWA_HD] = (acc / den).astype(BF16)


def _swa_attn(sink, q_hm, k_hm, v_hm, kc_s, vc_s):
    out_shape = jax.ShapeDtypeStruct((NT_TOK, SWA_QW), BF16)
    smem = pl.BlockSpec(memory_space=pltpu.SMEM)
    gw = SWA_GROUP * SWA_HD
    o = pl.pallas_call(
        functools.partial(_swa_attn_kernel, local=False),
        grid=(SWA_KV_HEADS, N_PROMPT_SEQ),
        in_specs=[smem,
                  pl.BlockSpec((SWA_GROUP, PROMPT_LEN, SWA_HD), lambda kv, b: (kv, b, 0)),
                  pl.BlockSpec((1, PROMPT_LEN, SWA_HD), lambda kv, b: (kv, b, 0)),
                  pl.BlockSpec((1, PROMPT_LEN, SWA_HD), lambda kv, b: (kv, b, 0))],
        out_specs=pl.BlockSpec((PROMPT_LEN, gw), lambda kv, b: (b, kv)),
        out_shape=out_shape,
        compiler_params=_cparams(("parallel", "parallel")),
        name="swa_attn_prompt",
    )(sink, q_hm, k_hm, v_hm)
    nb = SAMPLE_LEN // SWA_BLOCK
    b0 = NP_TOK // SWA_BLOCK

    def loc(off):
        return pl.BlockSpec((1, SWA_BLOCK, SWA_HD),
                            lambda kv, b, j: (kv, b0 + b * nb + jnp.clip(j + off, 0, nb - 1), 0))

    return pl.pallas_call(
        functools.partial(_swa_attn_kernel, local=True),
        grid=(SWA_KV_HEADS, N_SAMPLE_SEQ, nb),
        in_specs=[smem,
                  pl.BlockSpec((SWA_GROUP, SWA_BLOCK, SWA_HD), lambda kv, b, j: (kv, b0 + b * nb + j, 0)),
                  pl.BlockSpec((1, 1, PAST_LEN, SWA_HD), lambda kv, b, j: (b, kv, 0, 0)),
                  pl.BlockSpec((1, 1, PAST_LEN, SWA_HD), lambda kv, b, j: (b, kv, 0, 0)),
                  loc(-1), loc(0), loc(1), loc(-1), loc(0), loc(1),
                  pl.BlockSpec(memory_space=pl.ANY)],
        out_specs=pl.BlockSpec((SWA_BLOCK, gw), lambda kv, b, j: (b0 + b * nb + j, kv)),
        out_shape=out_shape,
        input_output_aliases={10: 0},
        compiler_params=_cparams(("parallel", "parallel", "arbitrary")),
        name="swa_attn_sample",
    )(sink, q_hm, kc_s, vc_s, k_hm, k_hm, k_hm, v_hm, v_hm, v_hm, o)


def _rope_tables():
    t = jnp.arange(SAMPLE_LEN, dtype=jnp.int32)
    inv = ROPE_BASE ** (-jnp.arange(16, dtype=F32) / 16)
    ang_r = (t // GRID_W).astype(F32)[:, None] * inv[None, :]
    ang_c = (t % GRID_W).astype(F32)[:, None] * inv[None, :]
    cos = jnp.concatenate([jnp.cos(ang_r), jnp.cos(ang_r), jnp.cos(ang_c), jnp.cos(ang_c)], axis=-1)
    sin = jnp.concatenate([-jnp.sin(ang_r), jnp.sin(ang_r), -jnp.sin(ang_c), jnp.sin(ang_c)], axis=-1)
    cos = jnp.concatenate([cos, cos], axis=-1)
    sin = jnp.concatenate([sin, sin], axis=-1)
    cos_t = jnp.concatenate([jnp.ones((NP_TOK, 128), F32), cos], axis=0)
    sin_t = jnp.concatenate([jnp.zeros((NP_TOK, 128), F32), sin], axis=0)
    return cos_t, sin_t


def kernel(x_prompt, x_sample, c, state_hgrn, cache_mla_ckv, cache_mla_kpe, cache_swa_k, cache_swa_v, c_ctx, ada_w, ada_b, norm_g, mlp_w_in, mlp_w_out, hgrn_w_in, hgrn_lb_logits, hgrn_norm_g, hgrn_w_out, mla_w_down, mla_q_norm_g, mla_kv_norm_g, mla_w_uq, mla_w_ukv, mla_w_out, swa_w_qkv, swa_sink, swa_w_out):
    y = jnp.concatenate([x_prompt.reshape(NP_TOK, D_MODEL), x_sample.reshape(NS_TOK, D_MODEL)], axis=0)
    cond8 = jnp.concatenate([c_ctx[None, :], c, jnp.zeros((3, D_MODEL), F32)], axis=0)
    mods = _adaln(cond8, ada_w, ada_b).reshape(DEPTH, 8, N_MOD, 1, D_MODEL)
    cos_t, sin_t = _rope_tables()

    lb_soft = jax.nn.softmax(hgrn_lb_logits.astype(F32), axis=1)
    lb_all = jnp.cumsum(lb_soft, axis=1) - lb_soft[:, :1]

    new_hgrn = []
    new_ckv = new_kpe = new_k = new_v = None
    for layer in range(DEPTH):
        kind, j = layer % 3, layer // 3
        norm_l = norm_g[layer].reshape(4, 1, D_MODEL)
        mods_l = mods[layer]
        if kind == 0:
            p_hm = _hgrn_in(y, norm_l, mods_l, hgrn_w_in[j].astype(BF16))
            lb = lb_all[:, layer].reshape(2, HGRN_HEADS, 1, HGRN_HD)
            o_f, s_f = _hgrn_scan(p_hm, lb[0], state_hgrn[:, j, 0], rev=False)
            o_b, s_b = _hgrn_scan(p_hm, lb[1], state_hgrn[:, j, 1], rev=True)
            new_hgrn.append(jnp.stack([s_f[:N_PROMPT_SEQ], s_b[:N_PROMPT_SEQ]], axis=1))
            y = _hgrn_out(o_f, o_b, p_hm, hgrn_norm_g[j].reshape(1, HGRN_HD), y, norm_l, mods_l,
                          hgrn_w_out[j].astype(BF16))
        elif kind == 1:
            w_down = jnp.pad(mla_w_down[j], ((0, 0), (0, MLA_DOWN_PAD - mla_w_down.shape[-1]))).astype(BF16)
            cq, ckv, kpe, kpe128 = _mla_down(y, norm_l, mods_l, w_down, mla_q_norm_g[j].reshape(1, -1),
                                             mla_kv_norm_g[j].reshape(1, -1), cos_t, sin_t)
            w_uq = mla_w_uq[j].reshape(MLA_Q_LORA, MLA_HEADS, MLA_QK).transpose(1, 0, 2)
            w_uq = jnp.pad(w_uq, ((0, 0), (0, 0), (0, MLA_QK_PAD - MLA_QK))).astype(BF16)
            w_ukv = mla_w_ukv[j].reshape(MLA_KV_LORA, MLA_HEADS, MLA_NOPE + MLA_V).transpose(1, 0, 2).astype(BF16)
            q_hm = _mla_q(cq, w_uq, cos_t, sin_t)
            ckv_s = jnp.concatenate([cache_mla_ckv[:, j].astype(BF16),
                                     ckv[NP_TOK:].reshape(N_SAMPLE_SEQ, SAMPLE_LEN, MLA_KV_LORA).astype(BF16)], axis=1)
            kpe_ctx = jnp.pad(cache_mla_kpe[:, j], ((0, 0), (0, 0), (0, 128 - MLA_ROPE))).astype(BF16)
            kpe_s = jnp.concatenate([kpe_ctx, kpe128[NP_TOK:].reshape(N_SAMPLE_SEQ, SAMPLE_LEN, 128)], axis=1)
            ckv_all = jnp.concatenate([ckv_s.reshape(-1, MLA_KV_LORA), ckv[:NP_TOK].astype(BF16)], axis=0)
            kpe_all = jnp.concatenate([kpe_s.reshape(-1, 128), kpe128[:NP_TOK]], axis=0)
            k_hm, v_hm = _mla_kv(ckv_all, kpe_all, w_ukv)
            o = _mla_attn(q_hm, k_hm, v_hm)
            new_ckv = ckv[:NP_TOK].reshape(N_PROMPT_SEQ, 1, PROMPT_LEN, MLA_KV_LORA)
            new_kpe = kpe[:NP_TOK].reshape(N_PROMPT_SEQ, 1, PROMPT_LEN, MLA_ROPE)
            y = _outproj(o, y, norm_l, mods_l, mla_w_out[j].astype(BF16))
        else:
            q_hm, k_hm, v_hm, kf, vf = _swa_qkv(y, norm_l, mods_l, swa_w_qkv[j].astype(BF16), cos_t, sin_t)
            kc_s = cache_swa_k[:, j].transpose(0, 2, 1, 3).astype(BF16)
            vc_s = cache_swa_v[:, j].transpose(0, 2, 1, 3).astype(BF16)
            o = _swa_attn(swa_sink[j], q_hm, k_hm, v_hm, kc_s, vc_s)
            new_k = kf[:NP_TOK].reshape(N_PROMPT_SEQ, 1, PROMPT_LEN, SWA_KV_HEADS, SWA_HD)
            new_v = vf[:NP_TOK].reshape(N_PROMPT_SEQ, 1, PROMPT_LEN, SWA_KV_HEADS, SWA_HD)
            y = _outproj(o, y, norm_l, mods_l, swa_w_out[j].astype(BF16))
        y = _mlp(y, norm_l, mods_l, mlp_w_in[layer].astype(BF16), mlp_w_out[layer].astype(BF16))

    y_prompt = y[:NP_TOK].reshape(N_PROMPT_SEQ, PROMPT_LEN, D_MODEL)
    y_sample = y[NP_TOK:].reshape(N_SAMPLE_SEQ, SAMPLE_LEN, D_MODEL)
    return (y_prompt, y_sample, jnp.stack(new_hgrn, axis=1), new_ckv, new_kpe, new_k, new_v)
```

```python
import functools

import jax
import jax.numpy as jnp
from jax import lax
from jax.experimental import pallas as pl
from jax.experimental.pallas import tpu as pltpu

F32 = jnp.float32
BF16 = jnp.bfloat16

D_MODEL = 2048
DEPTH = 4
N_PROMPT_SEQ = 16
PROMPT_LEN = 256
N_SAMPLE_SEQ = 4
SAMPLE_LEN = 4096
PAST_LEN = 256
NP_TOK = N_PROMPT_SEQ * PROMPT_LEN
NS_TOK = N_SAMPLE_SEQ * SAMPLE_LEN
NT_TOK = NP_TOK + NS_TOK
GROUP_TOK = 4096
N_GROUPS = NT_TOK // GROUP_TOK
N_MOD = 6
D_FF = 4 * D_MODEL
NORM_EPS = 1e-6
GRID_W = 64
ROPE_BASE = 10000.0

HGRN_HEADS = 16
HGRN_HD = 128
HGRN_CHUNK = 128
HGRN_BLOCK = 256
HGRN_DIAG = 8
HGRN_UNROLL = 8

MLA_HEADS = 16
MLA_Q_LORA = 512
MLA_KV_LORA = 512
MLA_NOPE = 128
MLA_ROPE = 64
MLA_V = 128
MLA_QK = MLA_NOPE + MLA_ROPE
MLA_QK_PAD = 256
MLA_DOWN_PAD = 1152
MLA_KV_LEN = PAST_LEN + SAMPLE_LEN
MLA_KEY_CHUNKS = (0, MLA_KV_LEN)
MLA_Q_SCALE = MLA_QK ** -0.5 * 1.4426950408889634

SWA_HD = 64
SWA_Q_HEADS = 32
SWA_KV_HEADS = 8
SWA_GROUP = SWA_Q_HEADS // SWA_KV_HEADS
SWA_QW = SWA_Q_HEADS * SWA_HD
SWA_KVW = SWA_KV_HEADS * SWA_HD
SWA_BLOCK = 128
NEG_INF = -1e30

TM = 512
TILES_PER_GROUP = GROUP_TOK // TM
VMEM_LIMIT = 56 * 1024 * 1024


def _cparams(sem):
    return pltpu.CompilerParams(dimension_semantics=sem, vmem_limit_bytes=VMEM_LIMIT)


def _dot(a, b):
    return jnp.dot(a, b, preferred_element_type=F32)


def _dot_nt(a, b):
    return lax.dot_general(a, b, (((1,), (1,)), ((), ())), preferred_element_type=F32)


def _rms(x, g):
    return x * lax.rsqrt(jnp.mean(x * x, axis=-1, keepdims=True) + NORM_EPS) * g


def _silu(x):
    return x * jax.nn.sigmoid(x)


def _prenorm(y_ref, g_ref, sh_ref, sc_ref):
    return (_rms(y_ref[...], g_ref[0]) * (1.0 + sc_ref[0, 0]) + sh_ref[0, 0]).astype(BF16)


def _post(y_ref, acc, g_ref, gate_ref):
    return y_ref[...] + gate_ref[0, 0] * _rms(acc, g_ref[0])


def _rope128(x, cos, sin):
    lane = lax.broadcasted_iota(jnp.int32, x.shape, 1)
    first = (lane & 31) < 16
    swapped = jnp.where(first, pltpu.roll(x, 112, 1), pltpu.roll(x, 16, 1))
    return x * cos + swapped * sin


def _tok_spec(n):
    return pl.BlockSpec((TM, n), lambda i, *_: (i, 0))


def _norm_spec(k):
    return pl.BlockSpec((1, 1, D_MODEL), lambda i, *_: (k, 0, 0))


def _mod_spec(k):
    return pl.BlockSpec((1, 1, 1, D_MODEL), lambda i, *_: (i // TILES_PER_GROUP, k, 0, 0))


def _rope_spec():
    def idx(i, *_):
        return (jnp.where(i < TILES_PER_GROUP, i, TILES_PER_GROUP + i % TILES_PER_GROUP), 0)
    return pl.BlockSpec((TM, 128), idx)


def _resident(shape):
    nd = len(shape)
    return pl.BlockSpec(shape, lambda *_: (0,) * nd, pipeline_mode=pl.Buffered(1))


def _adaln_kernel(c_ref, w_ref, b_ref, o_ref):
    s = _silu(c_ref[...])
    s_hi = s.astype(BF16)
    s_lo = (s - s_hi.astype(F32)).astype(BF16)
    w = w_ref[0]
    w_hi = w.astype(BF16)
    w_lo = (w - w_hi.astype(F32)).astype(BF16)
    o_ref[0] = _dot(s_hi, w_hi) + _dot(s_lo, w_hi) + _dot(s_hi, w_lo) + b_ref[0]


def _adaln(cond8, ada_w, ada_b):
    tn = 1024
    n = N_MOD * D_MODEL
    return pl.pallas_call(
        _adaln_kernel,
        grid=(DEPTH, n // tn),
        in_specs=[
            pl.BlockSpec((8, D_MODEL), lambda l, j: (0, 0)),
            pl.BlockSpec((1, D_MODEL, tn), lambda l, j: (l, 0, j)),
            pl.BlockSpec((1, 1, tn), lambda l, j: (l, 0, j)),
        ],
        out_specs=pl.BlockSpec((1, 8, tn), lambda l, j: (l, 0, j)),
        out_shape=jax.ShapeDtypeStruct((DEPTH, 8, n), F32),
        compiler_params=_cparams(("parallel", "parallel")),
        name="adaln",
    )(cond8, ada_w, ada_b.reshape(DEPTH, 1, n))


def _mlp_kernel(y_ref, g2_ref, g3_ref, sh_ref, sc_ref, gate_ref, w1_ref, w2_ref, o_ref, h_ref):
    f = pl.program_id(1)

    @pl.when(f == 0)
    def _():
        h_ref[...] = _prenorm(y_ref, g2_ref, sh_ref, sc_ref)
        o_ref[...] = jnp.zeros_like(o_ref)

    u = jnp.maximum(_dot(h_ref[...], w1_ref[...]), 0.0)
    o_ref[...] += _dot((u * u).astype(BF16), w2_ref[...])

    @pl.when(f == pl.num_programs(1) - 1)
    def _():
        o_ref[...] = _post(y_ref, o_ref[...], g3_ref, gate_ref)


def _mlp(y, norm_l, mods_l, w1, w2):
    tf = 1024
    return pl.pallas_call(
        _mlp_kernel,
        grid=(NT_TOK // TM, D_FF // tf),
        in_specs=[
            _tok_spec(D_MODEL), _norm_spec(2), _norm_spec(3),
            _mod_spec(3), _mod_spec(4), _mod_spec(5),
            pl.BlockSpec((D_MODEL, tf), lambda i, f: (0, f)),
            pl.BlockSpec((tf, D_MODEL), lambda i, f: (f, 0)),
        ],
        out_specs=_tok_spec(D_MODEL),
        out_shape=jax.ShapeDtypeStruct((NT_TOK, D_MODEL), F32),
        scratch_shapes=[pltpu.VMEM((TM, D_MODEL), BF16)],
        compiler_params=_cparams(("parallel", "arbitrary")),
        name="mlp",
    )(y, norm_l, norm_l, mods_l, mods_l, mods_l, w1, w2)


def _outproj_kernel(x_ref, y_ref, g1_ref, gate_ref, w_ref, o_ref):
    o_ref[...] = _post(y_ref, _dot(x_ref[...], w_ref[...]), g1_ref, gate_ref)


def _outproj(x, y, norm_l, mods_l, w):
    return pl.pallas_call(
        _outproj_kernel,
        grid=(NT_TOK // TM,),
        in_specs=[_tok_spec(D_MODEL), _tok_spec(D_MODEL), _norm_spec(1), _mod_spec(2),
                  _resident((D_MODEL, D_MODEL))],
        out_specs=_tok_spec(D_MODEL),
        out_shape=jax.ShapeDtypeStruct((NT_TOK, D_MODEL), F32),
        compiler_params=_cparams(("parallel",)),
        name="outproj",
    )(x, y, norm_l, mods_l, w)


def _hgrn_in_kernel(y_ref, g0_ref, sh_ref, sc_ref, w_ref, o_ref, h_ref):
    @pl.when(pl.program_id(1) == 0)
    def _():
        h_ref[...] = _prenorm(y_ref, g0_ref, sh_ref, sc_ref)

    acc = _dot(h_ref[...], w_ref[...])
    for c in range(HGRN_HEADS):
        o_ref[0, c] = acc[:, c * HGRN_HD:(c + 1) * HGRN_HD].astype(BF16)


def _hgrn_in(y, norm_l, mods_l, w):
    return pl.pallas_call(
        _hgrn_in_kernel,
        grid=(NT_TOK // TM, 5),
        in_specs=[_tok_spec(D_MODEL), _norm_spec(0), _mod_spec(0), _mod_spec(1),
                  pl.BlockSpec((D_MODEL, D_MODEL), lambda i, j: (0, j))],
        out_specs=pl.BlockSpec((1, HGRN_HEADS, TM, HGRN_HD), lambda i, j: (j, 0, i, 0)),
        out_shape=jax.ShapeDtypeStruct((5, HGRN_HEADS, NT_TOK, HGRN_HD), BF16),
        scratch_shapes=[pltpu.VMEM((TM, D_MODEL), BF16)],
        compiler_params=_cparams(("parallel", "arbitrary")),
        name="hgrn_in",
    )(y, norm_l, mods_l, mods_l, w)


HGRN_NBLK = NT_TOK // HGRN_BLOCK
HGRN_PBLK = NP_TOK // HGRN_BLOCK
HGRN_SBLK = SAMPLE_LEN // HGRN_BLOCK
HGRN_NSEQ = N_PROMPT_SEQ + N_SAMPLE_SEQ


def _hgrn_seq_of_block(blk):
    return jnp.where(blk < HGRN_PBLK, blk, HGRN_PBLK + (blk - HGRN_PBLK) // HGRN_SBLK)


def _hgrn_scan_kernel(zq_ref, zf_ref, zv_ref, lb_ref, s0_ref, o_ref, sfin_ref, st_ref, *, rev):
    c = HGRN_CHUNK
    nsub = HGRN_BLOCK // c
    i = pl.program_id(0)
    blk = (HGRN_NBLK - 1 - i) if rev else i
    is_prompt = blk < HGRN_PBLK
    pos = (blk - HGRN_PBLK) % HGRN_SBLK
    seq_start = jnp.logical_or(is_prompt, pos == (HGRN_SBLK - 1 if rev else 0))
    seq_end = jnp.logical_or(is_prompt, pos == (0 if rev else HGRN_SBLK - 1))

    @pl.when(jnp.logical_and(seq_start, is_prompt))
    def _():
        st_ref[...] = jnp.zeros_like(st_ref)

    @pl.when(jnp.logical_and(seq_start, jnp.logical_not(is_prompt)))
    def _():
        def init(h, carry):
            st_ref[h] = s0_ref[0, h].T
            return carry
        lax.fori_loop(0, HGRN_HEADS, init, 0)

    row = lax.broadcasted_iota(jnp.int32, (c, HGRN_HD), 0)
    tt = lax.broadcasted_iota(jnp.int32, (c, c), 0)
    ss = lax.broadcasted_iota(jnp.int32, (c, c), 1)
    tri = ((ss >= tt) if rev else (ss <= tt)).astype(BF16)
    same_tile = (tt >> 3) == (ss >> 3)
    bands = [jnp.logical_and(same_tile, ss == ((tt + d) if rev else (tt - d))) for d in range(HGRN_DIAG)]
    levels = []
    size = 2 * HGRN_DIAG
    while size <= c:
        half = size // 2
        sh = size.bit_length() - 1
        row_hi = (row & (size - 1)) >= half
        t_hi = (tt & (size - 1)) >= half
        s_hi = (ss & (size - 1)) >= half
        if rev:
            is_query = jnp.logical_not(row_hi)
            pair = jnp.logical_and(jnp.logical_not(t_hi), s_hi)
        else:
            is_query = row_hi
            pair = jnp.logical_and(t_hi, jnp.logical_not(s_hi))
        pair = jnp.logical_and(pair, (tt >> sh) == (ss >> sh))
        levels.append((size, is_query, jnp.where(is_query, 1.0, -1.0), pair))
        size *= 2

    def shift1(x):
        x3 = x.reshape(c // HGRN_DIAG, HGRN_DIAG, HGRN_HD)
        return pltpu.roll(x3, (HGRN_DIAG - 1) if rev else 1, 1).reshape(c, HGRN_HD)

    def chunk_math(zq, zf, v, lb, st):
        q = _silu(zq) * (HGRN_HD ** -0.5)
        f = lb + (1.0 - lb) * jax.nn.sigmoid(zf)
        k = 1.0 - f
        g = jnp.log(f)
        g1 = g.astype(BF16)
        g2 = (g - g1.astype(F32)).astype(BF16)
        cum = _dot(tri, g1) + _dot(tri, g2)
        tot = cum[0:1, :] if rev else cum[c - 1:c, :]

        z = k
        att = jnp.where(bands[0], jnp.sum(q * z, axis=-1, keepdims=True), 0.0)
        for d in range(1, HGRN_DIAG):
            z = shift1(z) * f
            att = jnp.where(bands[d], jnp.sum(q * z, axis=-1, keepdims=True), att)

        for size, is_query, sign, pair in levels:
            half = size // 2
            pieces = []
            for b in range(c // size):
                ref_row = b * size + (half if rev else half - 1)
                pieces.append(jnp.broadcast_to(cum[ref_row:ref_row + 1, :], (size, HGRN_HD)))
            ref = pieces[0] if len(pieces) == 1 else jnp.concatenate(pieces, axis=0)
            x = (jnp.where(is_query, q, k) * jnp.exp((cum - ref) * sign)).astype(BF16)
            att = jnp.where(pair, _dot_nt(x, x), att)

        o = _dot_nt((q * jnp.exp(cum)).astype(BF16), st.astype(BF16)) + _dot(att.astype(BF16), v.astype(BF16))

        ke = (k * jnp.exp(tot - cum)).astype(BF16)
        st_new = jnp.exp(tot) * st + _dot(v.T.astype(BF16), ke)
        return o.astype(BF16), st_new

    ngrp = HGRN_HEADS // HGRN_UNROLL

    def body(step, carry):
        ci = step // ngrp
        ci = (nsub - 1 - ci) if rev else ci
        h0 = (step % ngrp) * HGRN_UNROLL
        r0 = pl.multiple_of(ci * c, c)
        rows = pl.ds(r0, c)
        ins = []
        for u in range(HGRN_UNROLL):
            h = h0 + u
            ins.append((zq_ref[0, h, rows, :].astype(F32), zf_ref[0, h, rows, :].astype(F32),
                        zv_ref[0, h, rows, :].astype(F32), lb_ref[h], st_ref[h]))
        outs = [chunk_math(*a) for a in ins]
        for u in range(HGRN_UNROLL):
            o_ref[h0 + u, rows, :] = outs[u][0]
            st_ref[h0 + u] = outs[u][1]
        return carry

    lax.fori_loop(0, nsub * ngrp, body, 0)

    @pl.when(seq_end)
    def _():
        def fin(h, carry):
            sfin_ref[0, h] = st_ref[h].T
            return carry
        lax.fori_loop(0, HGRN_HEADS, fin, 0)


def _hgrn_scan(p_hm, lb_d, s0_d, rev):
    def blk_of(i):
        return (HGRN_NBLK - 1 - i) if rev else i

    def sec_spec(sec):
        return pl.BlockSpec((1, HGRN_HEADS, HGRN_BLOCK, HGRN_HD), lambda i: (sec, 0, blk_of(i), 0))

    def seq_idx(i):
        return _hgrn_seq_of_block(blk_of(i))

    return pl.pallas_call(
        functools.partial(_hgrn_scan_kernel, rev=rev),
        grid=(HGRN_NBLK,),
        in_specs=[
            sec_spec(0), sec_spec(2 if rev else 1), sec_spec(3),
            pl.BlockSpec((HGRN_HEADS, 1, HGRN_HD), lambda i: (0, 0, 0)),
            pl.BlockSpec((1, HGRN_HEADS, HGRN_HD, HGRN_HD),
                         lambda i: (jnp.maximum(seq_idx(i) - N_PROMPT_SEQ, 0), 0, 0, 0)),
        ],
        out_specs=[
            pl.BlockSpec((HGRN_HEADS, HGRN_BLOCK, HGRN_HD), lambda i: (0, blk_of(i), 0)),
            pl.BlockSpec((1, HGRN_HEADS, HGRN_HD, HGRN_HD), lambda i: (seq_idx(i), 0, 0, 0)),
        ],
        out_shape=[
            jax.ShapeDtypeStruct((HGRN_HEADS, NT_TOK, HGRN_HD), BF16),
            jax.ShapeDtypeStruct((HGRN_NSEQ, HGRN_HEADS, HGRN_HD, HGRN_HD), F32),
        ],
        scratch_shapes=[pltpu.VMEM((HGRN_HEADS, HGRN_HD, HGRN_HD), F32)],
        compiler_params=_cparams(("arbitrary",)),
        name="hgrn_scan_bwd" if rev else "hgrn_scan_fwd",
    )(p_hm, p_hm, p_hm, lb_d, s0_d)


def _hgrn_out_kernel(of_ref, ob_ref, zg_ref, ng_ref, y_ref, g1_ref, gate_ref, w_ref, o_ref, x_ref):
    ng = ng_ref[...]
    for h in range(HGRN_HEADS):
        o = of_ref[h].astype(F32) + ob_ref[h].astype(F32)
        x_ref[:, h * HGRN_HD:(h + 1) * HGRN_HD] = (_rms(o, ng) * _silu(zg_ref[0, h].astype(F32))).astype(BF16)
    o_ref[...] = _post(y_ref, _dot(x_ref[...], w_ref[...]), g1_ref, gate_ref)


def _hgrn_out(o_f, o_b, p_hm, hgrn_norm_g, y, norm_l, mods_l, w):
    head_spec = pl.BlockSpec((HGRN_HEADS, TM, HGRN_HD), lambda i: (0, i, 0))
    return pl.pallas_call(
        _hgrn_out_kernel,
        grid=(NT_TOK // TM,),
        in_specs=[
            head_spec, head_spec,
            pl.BlockSpec((1, HGRN_HEADS, TM, HGRN_HD), lambda i: (4, 0, i, 0)),
            pl.BlockSpec((1, HGRN_HD), lambda i: (0, 0)),
            _tok_spec(D_MODEL), _norm_spec(1), _mod_spec(2),
            _resident((D_MODEL, D_MODEL)),
        ],
        out_specs=_tok_spec(D_MODEL),
        out_shape=jax.ShapeDtypeStruct((NT_TOK, D_MODEL), F32),
        scratch_shapes=[pltpu.VMEM((TM, D_MODEL), BF16)],
        compiler_params=_cparams(("parallel",)),
        name="hgrn_out",
    )(o_f, o_b, p_hm, hgrn_norm_g, y, norm_l, mods_l, w)


def _mla_down_kernel(y_ref, g0_ref, sh_ref, sc_ref, w_ref, qg_ref, kvg_ref, cos_ref, sin_ref,
                     cq_ref, ckv_ref, kpe_ref, kpe128_ref):
    p = _dot(_prenorm(y_ref, g0_ref, sh_ref, sc_ref), w_ref[...])
    cq_ref[...] = _rms(p[:, :MLA_Q_LORA], qg_ref[...]).astype(BF16)
    ckv_ref[...] = _rms(p[:, MLA_Q_LORA:MLA_Q_LORA + MLA_KV_LORA], kvg_ref[...])
    tail = p[:, MLA_Q_LORA + MLA_KV_LORA:]
    kpe_ref[...] = tail[:, :MLA_ROPE]
    kpe128_ref[...] = _rope128(tail, cos_ref[...], sin_ref[...]).astype(BF16)


def _mla_down(y, norm_l, mods_l, w, qg, kvg, cos_t, sin_t):
    return pl.pallas_call(
        _mla_down_kernel,
        grid=(NT_TOK // TM,),
        in_specs=[_tok_spec(D_MODEL), _norm_spec(0), _mod_spec(0), _mod_spec(1),
                  _resident((D_MODEL, MLA_DOWN_PAD)),
                  pl.BlockSpec((1, MLA_Q_LORA), lambda i: (0, 0)),
                  pl.BlockSpec((1, MLA_KV_LORA), lambda i: (0, 0)),
                  _rope_spec(), _rope_spec()],
        out_specs=[_tok_spec(MLA_Q_LORA), _tok_spec(MLA_KV_LORA), _tok_spec(MLA_ROPE), _tok_spec(128)],
        out_shape=[
            jax.ShapeDtypeStruct((NT_TOK, MLA_Q_LORA), BF16),
            jax.ShapeDtypeStruct((NT_TOK, MLA_KV_LORA), F32),
            jax.ShapeDtypeStruct((NT_TOK, MLA_ROPE), F32),
            jax.ShapeDtypeStruct((NT_TOK, 128), BF16),
        ],
        compiler_params=_cparams(("parallel",)),
        name="mla_down",
    )(y, norm_l, mods_l, mods_l, w, qg, kvg, cos_t, sin_t)


MLA_HB = 4


def _mla_q_kernel(cq_ref, w_ref, cos_ref, sin_ref, q_ref):
    acc = _dot(cq_ref[...], w_ref[0]) * MLA_Q_SCALE
    cos = cos_ref[...]
    sin = sin_ref[...]
    for hh in range(MLA_HB):
        c0 = hh * MLA_QK_PAD
        q_ref[hh, :, :MLA_NOPE] = acc[:, c0:c0 + MLA_NOPE].astype(BF16)
        q_ref[hh, :, MLA_NOPE:] = _rope128(acc[:, c0 + MLA_NOPE:c0 + MLA_QK_PAD], cos, sin).astype(BF16)


def _mla_q(cq, w_uq_g, cos_t, sin_t):
    return pl.pallas_call(
        _mla_q_kernel,
        grid=(NT_TOK // TM, MLA_HEADS // MLA_HB),
        in_specs=[_tok_spec(MLA_Q_LORA),
                  pl.BlockSpec((1, MLA_Q_LORA, MLA_HB * MLA_QK_PAD), lambda i, h: (h, 0, 0)),
                  _rope_spec(), _rope_spec()],
        out_specs=pl.BlockSpec((MLA_HB, TM, MLA_QK_PAD), lambda i, h: (h, i, 0)),
        out_shape=jax.ShapeDtypeStruct((MLA_HEADS, NT_TOK, MLA_QK_PAD), BF16),
        compiler_params=_cparams(("parallel", "arbitrary")),
        name="mla_q",
    )(cq, w_uq_g, cos_t, sin_t)


def _mla_kv_kernel(ckv_ref, kpe_ref, w_ref, k_ref, vt_ref):
    acc = _dot(ckv_ref[...], w_ref[0])
    kpe = kpe_ref[...]
    for hh in range(MLA_HB):
        c0 = hh * (MLA_NOPE + MLA_V)
        k_ref[hh, :, :MLA_NOPE] = acc[:, c0:c0 + MLA_NOPE].astype(BF16)
        k_ref[hh, :, MLA_NOPE:] = kpe
        vt_ref[hh] = acc[:, c0 + MLA_NOPE:c0 + MLA_NOPE + MLA_V].T.astype(BF16)


def _mla_kv(ckv_all, kpe_all, w_ukv_g):
    n = ckv_all.shape[0]
    return pl.pallas_call(
        _mla_kv_kernel,
        grid=(n // TM, MLA_HEADS // MLA_HB),
        in_specs=[_tok_spec(MLA_KV_LORA), _tok_spec(128),
                  pl.BlockSpec((1, MLA_KV_LORA, MLA_HB * (MLA_NOPE + MLA_V)), lambda i, h: (h, 0, 0))],
        out_specs=[pl.BlockSpec((MLA_HB, TM, MLA_QK_PAD), lambda i, h: (h, i, 0)),
                   pl.BlockSpec((MLA_HB, MLA_V, TM), lambda i, h: (h, 0, i))],
        out_shape=[jax.ShapeDtypeStruct((MLA_HEADS, n, MLA_QK_PAD), BF16),
                   jax.ShapeDtypeStruct((MLA_HEADS, MLA_V, n), BF16)],
        compiler_params=_cparams(("parallel", "arbitrary")),
        name="mla_kv",
    )(ckv_all, kpe_all, w_ukv_g)


def _row_reduce(x, reduce_fn, combine_fn):
    rows = x.shape[0]
    parts = ROW_REDUCE_PARTS
    while rows % (8 * parts):
        parts //= 2
    step = rows // parts
    acc = [reduce_fn(x[i * step:(i + 1) * step], axis=0, keepdims=True) for i in range(parts)]
    while len(acc) > 1:
        acc = [combine_fn(a, b) for a, b in zip(acc[0::2], acc[1::2])]
    return acc[0]


def _mla_attn_kernel(q_ref, k_ref, vt_ref, *rest, bounds):
    o_ref = rest[-1]
    q = q_ref[0]
    m = den = acc = None
    for lo, hi in zip(bounds[:-1], bounds[1:]):
        s = _dot_nt(k_ref[0, lo:hi, :], q)
        mc = _row_reduce(s, jnp.max, jnp.maximum)
        if m is None:
            m = mc
            p = jnp.exp2(s - m)
            den = _row_reduce(p, jnp.sum, jnp.add)
            acc = _dot(vt_ref[0, :, lo:hi], p.astype(BF16))
        else:
            m_new = jnp.maximum(m, mc)
            alpha = jnp.exp2(m - m_new)
            p = jnp.exp2(s - m_new)
            den = alpha * den + _row_reduce(p, jnp.sum, jnp.add)
            acc = alpha * acc + _dot(vt_ref[0, :, lo:hi], p.astype(BF16))
            m = m_new
    o_ref[...] = (acc / den).T.astype(BF16)


MLA_TQ = 512
ROW_REDUCE_PARTS = 16


def _mla_attn(q_hm, k_hm, vt_hm):
    tq = MLA_TQ
    out_shape = jax.ShapeDtypeStruct((NT_TOK, MLA_HEADS * MLA_V), BF16)
    pk0 = N_SAMPLE_SEQ * MLA_KV_LEN // PROMPT_LEN
    o = pl.pallas_call(
        functools.partial(_mla_attn_kernel, bounds=(0, PROMPT_LEN)),
        grid=(MLA_HEADS, N_PROMPT_SEQ),
        in_specs=[pl.BlockSpec((1, PROMPT_LEN, MLA_QK_PAD), lambda h, b: (h, b, 0)),
                  pl.BlockSpec((1, PROMPT_LEN, MLA_QK_PAD), lambda h, b: (h, pk0 + b, 0)),
                  pl.BlockSpec((1, MLA_V, PROMPT_LEN), lambda h, b: (h, 0, pk0 + b))],
        out_specs=pl.BlockSpec((PROMPT_LEN, MLA_V), lambda h, b: (b, h)),
        out_shape=out_shape,
        compiler_params=_cparams(("parallel", "parallel")),
        name="mla_attn_prompt",
    )(q_hm, k_hm, vt_hm)
    q0 = NP_TOK // tq
    nq = SAMPLE_LEN // tq
    return pl.pallas_call(
        functools.partial(_mla_attn_kernel, bounds=MLA_KEY_CHUNKS),
        grid=(MLA_HEADS, N_SAMPLE_SEQ, nq),
        in_specs=[pl.BlockSpec((1, tq, MLA_QK_PAD), lambda h, b, i: (h, q0 + b * nq + i, 0)),
                  pl.BlockSpec((1, MLA_KV_LEN, MLA_QK_PAD), lambda h, b, i: (h, b, 0)),
                  pl.BlockSpec((1, MLA_V, MLA_KV_LEN), lambda h, b, i: (h, 0, b)),
                  pl.BlockSpec(memory_space=pl.ANY)],
        out_specs=pl.BlockSpec((tq, MLA_V), lambda h, b, i: (q0 + b * nq + i, h)),
        out_shape=out_shape,
        input_output_aliases={3: 0},
        compiler_params=_cparams(("parallel", "parallel", "arbitrary")),
        name="mla_attn_sample",
    )(q_hm, k_hm, vt_hm, o)


def _swa_qkv_kernel(y_ref, g0_ref, sh_ref, sc_ref, w_ref, cos_ref, sin_ref, q_ref, k_ref, v_ref, kf_ref, vf_ref):
    p = _dot(_prenorm(y_ref, g0_ref, sh_ref, sc_ref), w_ref[...])
    cos = cos_ref[...]
    sin = sin_ref[...]
    vf_ref[...] = p[:, SWA_QW + SWA_KVW:]
    kf_ref[...] = p[:, SWA_QW:SWA_QW + SWA_KVW]
    for c in range((SWA_QW + SWA_KVW) // 128):
        r = _rope128(p[:, c * 128:(c + 1) * 128], cos, sin).astype(BF16)
        for half in range(2):
            head = 2 * c + half
            piece = r[:, half * SWA_HD:(half + 1) * SWA_HD]
            if head < SWA_Q_HEADS:
                q_ref[head] = piece
            else:
                k_ref[head - SWA_Q_HEADS] = piece
    for head in range(SWA_KV_HEADS):
        v_ref[head] = p[:, SWA_QW + SWA_KVW + head * SWA_HD:SWA_QW + SWA_KVW + (head + 1) * SWA_HD].astype(BF16)


def _swa_qkv(y, norm_l, mods_l, w, cos_t, sin_t):
    def hm(nh):
        return pl.BlockSpec((nh, TM, SWA_HD), lambda i: (0, i, 0))
    return pl.pallas_call(
        _swa_qkv_kernel,
        grid=(NT_TOK // TM,),
        in_specs=[_tok_spec(D_MODEL), _norm_spec(0), _mod_spec(0), _mod_spec(1),
                  _resident((D_MODEL, SWA_QW + 2 * SWA_KVW)), _rope_spec(), _rope_spec()],
        out_specs=[hm(SWA_Q_HEADS), hm(SWA_KV_HEADS), hm(SWA_KV_HEADS),
                   _tok_spec(SWA_KVW), _tok_spec(SWA_KVW)],
        out_shape=[
            jax.ShapeDtypeStruct((SWA_Q_HEADS, NT_TOK, SWA_HD), BF16),
            jax.ShapeDtypeStruct((SWA_KV_HEADS, NT_TOK, SWA_HD), BF16),
            jax.ShapeDtypeStruct((SWA_KV_HEADS, NT_TOK, SWA_HD), BF16),
            jax.ShapeDtypeStruct((NT_TOK, SWA_KVW), F32),
            jax.ShapeDtypeStruct((NT_TOK, SWA_KVW), F32),
        ],
        compiler_params=_cparams(("parallel",)),
        name="swa_qkv",
    )(y, norm_l, mods_l, mods_l, w, cos_t, sin_t)


def _swa_attn_kernel(sink_ref, q_ref, kc_ref, vc_ref, *rest, local):
    o_ref = rest[-1]
    kv = pl.program_id(0)
    tq = q_ref.shape[1]
    rows = SWA_GROUP * tq
    q = q_ref[...].reshape(rows, SWA_HD) * (SWA_HD ** -0.5)
    kc = kc_ref[...].reshape(kc_ref.shape[-2], SWA_HD)
    vc = vc_ref[...].reshape(vc_ref.shape[-2], SWA_HD)
    head = lax.broadcasted_iota(jnp.int32, (rows, 1), 0) >> (tq.bit_length() - 1)
    sink = jnp.zeros((rows, 1), F32)
    for g in range(SWA_GROUP):
        sink = jnp.where(head == g, sink_ref[kv * SWA_GROUP + g], sink)
    scores = [_dot_nt(q, kc)]
    values = [vc]
    if local:
        j = pl.program_id(2)
        nb = pl.num_programs(2)
        rr = lax.broadcasted_iota(jnp.int32, (rows, SWA_BLOCK), 0) & (SWA_BLOCK - 1)
        cc = lax.broadcasted_iota(jnp.int32, (rows, SWA_BLOCK), 1)
        keep = [jnp.logical_and(cc >= rr, j > 0), None, jnp.logical_and(cc <= rr, j < nb - 1)]
        for t in range(3):
            s = _dot_nt(q, rest[t][0])
            scores.append(s if keep[t] is None else jnp.where(keep[t], s, NEG_INF))
            values.append(rest[3 + t][0])
    s = scores[0] if len(scores) == 1 else jnp.concatenate(scores, axis=1)
    v = values[0] if len(values) == 1 else jnp.concatenate(values, axis=0)
    m = jnp.maximum(jnp.max(s, axis=-1, keepdims=True), sink)
    p = jnp.exp(s - m)
    den = jnp.sum(p, axis=-1, keepdims=True) + jnp.exp(sink - m)
    out = (_dot(p.astype(BF16), v) / den).astype(BF16)
    for g in range(SWA_GROUP):
        o_ref[:, g * SWA_HD:(g + 1) * SWA_HD] = out[g * tq:(g + 1) * tq]


def _swa_attn(sink, q_hm, k_hm, v_hm, kc_s, vc_s):
    out_shape = jax.ShapeDtypeStruct((NT_TOK, SWA_QW), BF16)
    smem = pl.BlockSpec(memory_space=pltpu.SMEM)
    gw = SWA_GROUP * SWA_HD
    o = pl.pallas_call(
        functools.partial(_swa_attn_kernel, local=False),
        grid=(SWA_KV_HEADS, N_PROMPT_SEQ),
        in_specs=[smem,
                  pl.BlockSpec((SWA_GROUP, PROMPT_LEN, SWA_HD), lambda kv, b: (kv, b, 0)),
                  pl.BlockSpec((1, PROMPT_LEN, SWA_HD), lambda kv, b: (kv, b, 0)),
                  pl.BlockSpec((1, PROMPT_LEN, SWA_HD), lambda kv, b: (kv, b, 0))],
        out_specs=pl.BlockSpec((PROMPT_LEN, gw), lambda kv, b: (b, kv)),
        out_shape=out_shape,
        compiler_params=_cparams(("parallel", "parallel")),
        name="swa_attn_prompt",
    )(sink, q_hm, k_hm, v_hm)
    nb = SAMPLE_LEN // SWA_BLOCK
    b0 = NP_TOK // SWA_BLOCK

    def loc(off):
        return pl.BlockSpec((1, SWA_BLOCK, SWA_HD),
                            lambda kv, b, j: (kv, b0 + b * nb + jnp.clip(j + off, 0, nb - 1), 0))

    return pl.pallas_call(
        functools.partial(_swa_attn_kernel, local=True),
        grid=(SWA_KV_HEADS, N_SAMPLE_SEQ, nb),
        in_specs=[smem,
                  pl.BlockSpec((SWA_GROUP, SWA_BLOCK, SWA_HD), lambda kv, b, j: (kv, b0 + b * nb + j, 0)),
                  pl.BlockSpec((1, 1, PAST_LEN, SWA_HD), lambda kv, b, j: (b, kv, 0, 0)),
                  pl.BlockSpec((1, 1, PAST_LEN, SWA_HD), lambda kv, b, j: (b, kv, 0, 0)),
                  loc(-1), loc(0), loc(1), loc(-1), loc(0), loc(1),
                  pl.BlockSpec(memory_space=pl.ANY)],
        out_specs=pl.BlockSpec((SWA_BLOCK, gw), lambda kv, b, j: (b0 + b * nb + j, kv)),
        out_shape=out_shape,
        input_output_aliases={10: 0},
        compiler_params=_cparams(("parallel", "parallel", "arbitrary")),
        name="swa_attn_sample",
    )(sink, q_hm, kc_s, vc_s, k_hm, k_hm, k_hm, v_hm, v_hm, v_hm, o)


def _rope_tables():
    t = jnp.arange(SAMPLE_LEN, dtype=jnp.int32)
    inv = ROPE_BASE ** (-jnp.arange(16, dtype=F32) / 16)
    ang_r = (t // GRID_W).astype(F32)[:, None] * inv[None, :]
    ang_c = (t % GRID_W).astype(F32)[:, None] * inv[None, :]
    cos = jnp.concatenate([jnp.cos(ang_r), jnp.cos(ang_r), jnp.cos(ang_c), jnp.cos(ang_c)], axis=-1)
    sin = jnp.concatenate([-jnp.sin(ang_r), jnp.sin(ang_r), -jnp.sin(ang_c), jnp.sin(ang_c)], axis=-1)
    cos = jnp.concatenate([cos, cos], axis=-1)
    sin = jnp.concatenate([sin, sin], axis=-1)
    cos_t = jnp.concatenate([jnp.ones((NP_TOK, 128), F32), cos], axis=0)
    sin_t = jnp.concatenate([jnp.zeros((NP_TOK, 128), F32), sin], axis=0)
    return cos_t, sin_t


def kernel(x_prompt, x_sample, c, state_hgrn, cache_mla_ckv, cache_mla_kpe, cache_swa_k, cache_swa_v, c_ctx, ada_w, ada_b, norm_g, mlp_w_in, mlp_w_out, hgrn_w_in, hgrn_lb_logits, hgrn_norm_g, hgrn_w_out, mla_w_down, mla_q_norm_g, mla_kv_norm_g, mla_w_uq, mla_w_ukv, mla_w_out, swa_w_qkv, swa_sink, swa_w_out):
    y = jnp.concatenate([x_prompt.reshape(NP_TOK, D_MODEL), x_sample.reshape(NS_TOK, D_MODEL)], axis=0)
    cond8 = jnp.concatenate([c_ctx[None, :], c, jnp.zeros((3, D_MODEL), F32)], axis=0)
    mods = _adaln(cond8, ada_w, ada_b).reshape(DEPTH, 8, N_MOD, 1, D_MODEL)
    cos_t, sin_t = _rope_tables()

    lb_soft = jax.nn.softmax(hgrn_lb_logits.astype(F32), axis=1)
    lb_all = jnp.cumsum(lb_soft, axis=1) - lb_soft[:, :1]

    new_hgrn = []
    new_ckv = new_kpe = new_k = new_v = None
    for layer in range(DEPTH):
        kind, j = layer % 3, layer // 3
        norm_l = norm_g[layer].reshape(4, 1, D_MODEL)
        mods_l = mods[layer]
        if kind == 0:
            p_hm = _hgrn_in(y, norm_l, mods_l, hgrn_w_in[j].astype(BF16))
            lb = lb_all[:, layer].reshape(2, HGRN_HEADS, 1, HGRN_HD)
            o_f, s_f = _hgrn_scan(p_hm, lb[0], state_hgrn[:, j, 0], rev=False)
            o_b, s_b = _hgrn_scan(p_hm, lb[1], state_hgrn[:, j, 1], rev=True)
            new_hgrn.append(jnp.stack([s_f[:N_PROMPT_SEQ], s_b[:N_PROMPT_SEQ]], axis=1))
            y = _hgrn_out(o_f, o_b, p_hm, hgrn_norm_g[j].reshape(1, HGRN_HD), y, norm_l, mods_l,
                          hgrn_w_out[j].astype(BF16))
        elif kind == 1:
            w_down = jnp.pad(mla_w_down[j], ((0, 0), (0, MLA_DOWN_PAD - mla_w_down.shape[-1]))).astype(BF16)
            cq, ckv, kpe, kpe128 = _mla_down(y, norm_l, mods_l, w_down, mla_q_norm_g[j].reshape(1, -1),
                                             mla_kv_norm_g[j].reshape(1, -1), cos_t, sin_t)
            ng = MLA_HEADS // MLA_HB
            w_uq = jnp.pad(mla_w_uq[j].reshape(MLA_Q_LORA, MLA_HEADS, MLA_QK),
                           ((0, 0), (0, 0), (0, MLA_QK_PAD - MLA_QK)))
            w_uq = w_uq.reshape(MLA_Q_LORA, ng, MLA_HB * MLA_QK_PAD).transpose(1, 0, 2).astype(BF16)
            w_ukv = mla_w_ukv[j].reshape(MLA_KV_LORA, ng, MLA_HB * (MLA_NOPE + MLA_V)).transpose(1, 0, 2).astype(BF16)
            q_hm = _mla_q(cq, w_uq, cos_t, sin_t)
            ckv_s = jnp.concatenate([cache_mla_ckv[:, j].astype(BF16),
                                     ckv[NP_TOK:].reshape(N_SAMPLE_SEQ, SAMPLE_LEN, MLA_KV_LORA).astype(BF16)], axis=1)
            kpe_ctx = jnp.pad(cache_mla_kpe[:, j], ((0, 0), (0, 0), (0, 128 - MLA_ROPE))).astype(BF16)
            kpe_s = jnp.concatenate([kpe_ctx, kpe128[NP_TOK:].reshape(N_SAMPLE_SEQ, SAMPLE_LEN, 128)], axis=1)
            ckv_all = jnp.concatenate([ckv_s.reshape(-1, MLA_KV_LORA), ckv[:NP_TOK].astype(BF16)], axis=0)
            kpe_all = jnp.concatenate([kpe_s.reshape(-1, 128), kpe128[:NP_TOK]], axis=0)
            k_hm, v_hm = _mla_kv(ckv_all, kpe_all, w_ukv)
            o = _mla_attn(q_hm, k_hm, v_hm)
            new_ckv = ckv[:NP_TOK].reshape(N_PROMPT_SEQ, 1, PROMPT_LEN, MLA_KV_LORA)
            new_kpe = kpe[:NP_TOK].reshape(N_PROMPT_SEQ, 1, PROMPT_LEN, MLA_ROPE)
            y = _outproj(o, y, norm_l, mods_l, mla_w_out[j].astype(BF16))
        else:
            q_hm, k_hm, v_hm, kf, vf = _swa_qkv(y, norm_l, mods_l, swa_w_qkv[j].astype(BF16), cos_t, sin_t)
            kc_s = cache_swa_k[:, j].transpose(0, 2, 1, 3).astype(BF16)
            vc_s = cache_swa_v[:, j].transpose(0, 2, 1, 3).astype(BF16)
            o = _swa_attn(swa_sink[j], q_hm, k_hm, v_hm, kc_s, vc_s)
            new_k = kf[:NP_TOK].reshape(N_PROMPT_SEQ, 1, PROMPT_LEN, SWA_KV_HEADS, SWA_HD)
            new_v = vf[:NP_TOK].reshape(N_PROMPT_SEQ, 1, PROMPT_LEN, SWA_KV_HEADS, SWA_HD)
            y = _outproj(o, y, norm_l, mods_l, swa_w_out[j].astype(BF16))
        y = _mlp(y, norm_l, mods_l, mlp_w_in[layer].astype(BF16), mlp_w_out[layer].astype(BF16))

    y_prompt = y[:NP_TOK].reshape(N_PROMPT_SEQ, PROMPT_LEN, D_MODEL)
    y_sample = y[NP_TOK:].reshape(N_SAMPLE_SEQ, SAMPLE_LEN, D_MODEL)
    return (y_prompt, y_sample, jnp.stack(new_hgrn, axis=1), new_ckv, new_kpe, new_k, new_v)
```

```python
import functools

import jax
import jax.numpy as jnp
from jax import lax
from jax.experimental import pallas as pl
from jax.experimental.pallas import tpu as pltpu

F32 = jnp.float32
BF16 = jnp.bfloat16

D_MODEL = 2048
DEPTH = 4
N_PROMPT_SEQ = 16
PROMPT_LEN = 256
N_SAMPLE_SEQ = 4
SAMPLE_LEN = 4096
PAST_LEN = 256
NP_TOK = N_PROMPT_SEQ * PROMPT_LEN
NS_TOK = N_SAMPLE_SEQ * SAMPLE_LEN
NT_TOK = NP_TOK + NS_TOK
GROUP_TOK = 4096
N_GROUPS = NT_TOK // GROUP_TOK
N_MOD = 6
D_FF = 4 * D_MODEL
NORM_EPS = 1e-6
GRID_W = 64
ROPE_BASE = 10000.0

HGRN_HEADS = 16
HGRN_HD = 128
HGRN_CHUNK = 128
HGRN_BLOCK = 256
HGRN_DIAG = 8
HGRN_UNROLL = 8

MLA_HEADS = 16
MLA_Q_LORA = 512
MLA_KV_LORA = 512
MLA_NOPE = 128
MLA_ROPE = 64
MLA_V = 128
MLA_QK = MLA_NOPE + MLA_ROPE
MLA_QK_PAD = 256
MLA_DOWN_PAD = 1152
MLA_KV_LEN = PAST_LEN + SAMPLE_LEN
MLA_KEY_CHUNKS = (0, 640, 1280, 1792, 2304, 2816, 3328, 3840, MLA_KV_LEN)
MLA_Q_SCALE = MLA_QK ** -0.5 * 1.4426950408889634

SWA_HD = 64
SWA_Q_HEADS = 32
SWA_KV_HEADS = 8
SWA_GROUP = SWA_Q_HEADS // SWA_KV_HEADS
SWA_QW = SWA_Q_HEADS * SWA_HD
SWA_KVW = SWA_KV_HEADS * SWA_HD
SWA_BLOCK = 128
SWA_KVB = 8
NEG_INF = -1e30

TM = 512
TILES_PER_GROUP = GROUP_TOK // TM
VMEM_LIMIT = 56 * 1024 * 1024


def _cparams(sem):
    return pltpu.CompilerParams(dimension_semantics=sem, vmem_limit_bytes=VMEM_LIMIT)


def _dot(a, b):
    return jnp.dot(a, b, preferred_element_type=F32)


def _dot_nt(a, b):
    return lax.dot_general(a, b, (((1,), (1,)), ((), ())), preferred_element_type=F32)


def _rms(x, g):
    return x * lax.rsqrt(jnp.mean(x * x, axis=-1, keepdims=True) + NORM_EPS) * g


def _silu(x):
    return x * jax.nn.sigmoid(x)


def _prenorm(y_ref, g_ref, sh_ref, sc_ref):
    return (_rms(y_ref[...], g_ref[0]) * (1.0 + sc_ref[0, 0]) + sh_ref[0, 0]).astype(BF16)


def _post(y_ref, acc, g_ref, gate_ref):
    return y_ref[...] + gate_ref[0, 0] * _rms(acc, g_ref[0])


def _rope128(x, cos, sin):
    lane = lax.broadcasted_iota(jnp.int32, x.shape, 1)
    first = (lane & 31) < 16
    swapped = jnp.where(first, pltpu.roll(x, 112, 1), pltpu.roll(x, 16, 1))
    return x * cos + swapped * sin


def _tok_spec(n):
    return pl.BlockSpec((TM, n), lambda i, *_: (i, 0))


def _norm_spec(k):
    return pl.BlockSpec((1, 1, D_MODEL), lambda i, *_: (k, 0, 0))


def _mod_spec(k):
    return pl.BlockSpec((1, 1, 1, D_MODEL), lambda i, *_: (i // TILES_PER_GROUP, k, 0, 0))


def _rope_spec():
    def idx(i, *_):
        return (jnp.where(i < TILES_PER_GROUP, i, TILES_PER_GROUP + i % TILES_PER_GROUP), 0)
    return pl.BlockSpec((TM, 128), idx)


def _resident(shape):
    nd = len(shape)
    return pl.BlockSpec(shape, lambda *_: (0,) * nd, pipeline_mode=pl.Buffered(1))


def _adaln_kernel(c_ref, w_ref, b_ref, o_ref):
    s = _silu(c_ref[...])
    s_hi = s.astype(BF16)
    s_lo = (s - s_hi.astype(F32)).astype(BF16)
    w = w_ref[0]
    w_hi = w.astype(BF16)
    w_lo = (w - w_hi.astype(F32)).astype(BF16)
    o_ref[0] = _dot(s_hi, w_hi) + _dot(s_lo, w_hi) + _dot(s_hi, w_lo) + b_ref[0]


def _adaln(cond8, ada_w, ada_b):
    tn = 1024
    n = N_MOD * D_MODEL
    return pl.pallas_call(
        _adaln_kernel,
        grid=(DEPTH, n // tn),
        in_specs=[
            pl.BlockSpec((8, D_MODEL), lambda l, j: (0, 0)),
            pl.BlockSpec((1, D_MODEL, tn), lambda l, j: (l, 0, j)),
            pl.BlockSpec((1, 1, tn), lambda l, j: (l, 0, j)),
        ],
        out_specs=pl.BlockSpec((1, 8, tn), lambda l, j: (l, 0, j)),
        out_shape=jax.ShapeDtypeStruct((DEPTH, 8, n), F32),
        compiler_params=_cparams(("parallel", "parallel")),
        name="adaln",
    )(cond8, ada_w, ada_b.reshape(DEPTH, 1, n))


def _mlp_kernel(y_ref, g2_ref, g3_ref, sh_ref, sc_ref, gate_ref, w1_ref, w2_ref, o_ref, h_ref):
    f = pl.program_id(1)

    @pl.when(f == 0)
    def _():
        h_ref[...] = _prenorm(y_ref, g2_ref, sh_ref, sc_ref)
        o_ref[...] = jnp.zeros_like(o_ref)

    u = jnp.maximum(_dot(h_ref[...], w1_ref[...]), 0.0)
    o_ref[...] += _dot((u * u).astype(BF16), w2_ref[...])

    @pl.when(f == pl.num_programs(1) - 1)
    def _():
        o_ref[...] = _post(y_ref, o_ref[...], g3_ref, gate_ref)


def _mlp(y, norm_l, mods_l, w1, w2, tile0, ntiles):
    tf = 1024

    def mod(k):
        return pl.BlockSpec((1, 1, 1, D_MODEL), lambda i, f: ((i + tile0) // TILES_PER_GROUP, k, 0, 0))

    return pl.pallas_call(
        _mlp_kernel,
        grid=(ntiles, D_FF // tf),
        in_specs=[
            pl.BlockSpec((TM, D_MODEL), lambda i, f: (i + tile0, 0)), _norm_spec(2), _norm_spec(3),
            mod(3), mod(4), mod(5),
            pl.BlockSpec((D_MODEL, tf), lambda i, f: (0, f)),
            pl.BlockSpec((tf, D_MODEL), lambda i, f: (f, 0)),
        ],
        out_specs=_tok_spec(D_MODEL),
        out_shape=jax.ShapeDtypeStruct((ntiles * TM, D_MODEL), F32),
        scratch_shapes=[pltpu.VMEM((TM, D_MODEL), BF16)],
        compiler_params=_cparams(("parallel", "arbitrary")),
        name="mlp",
    )(y, norm_l, norm_l, mods_l, mods_l, mods_l, w1, w2)


PROMPT_TILES = NP_TOK // TM


def _outproj_kernel(xp_ref, xs_ref, y_ref, g1_ref, gate_ref, w_ref, o_ref):
    x = jnp.where(pl.program_id(0) < PROMPT_TILES, xp_ref[...], xs_ref[...])
    o_ref[...] = _post(y_ref, _dot(x, w_ref[...]), g1_ref, gate_ref)


def _outproj(x_p, x_s, y, norm_l, mods_l, w):
    return pl.pallas_call(
        _outproj_kernel,
        grid=(NT_TOK // TM,),
        in_specs=[pl.BlockSpec((TM, D_MODEL), lambda i: (jnp.minimum(i, PROMPT_TILES - 1), 0)),
                  pl.BlockSpec((TM, D_MODEL), lambda i: (jnp.maximum(i - PROMPT_TILES, 0), 0)),
                  _tok_spec(D_MODEL), _norm_spec(1), _mod_spec(2),
                  _resident((D_MODEL, D_MODEL))],
        out_specs=_tok_spec(D_MODEL),
        out_shape=jax.ShapeDtypeStruct((NT_TOK, D_MODEL), F32),
        compiler_params=_cparams(("parallel",)),
        name="outproj",
    )(x_p, x_s, y, norm_l, mods_l, w)


def _hgrn_in_kernel(y_ref, g0_ref, sh_ref, sc_ref, w_ref, o_ref, h_ref):
    @pl.when(pl.program_id(1) == 0)
    def _():
        h_ref[...] = _prenorm(y_ref, g0_ref, sh_ref, sc_ref)

    acc = _dot(h_ref[...], w_ref[...])

    @pl.when(pl.program_id(1) == 0)
    def _():
        q = _silu(acc) * (HGRN_HD ** -0.5)
        for c in range(HGRN_HEADS):
            o_ref[0, c] = q[:, c * HGRN_HD:(c + 1) * HGRN_HD].astype(BF16)

    @pl.when(pl.program_id(1) != 0)
    def _():
        for c in range(HGRN_HEADS):
            o_ref[0, c] = acc[:, c * HGRN_HD:(c + 1) * HGRN_HD].astype(BF16)


def _hgrn_in(y, norm_l, mods_l, w):
    return pl.pallas_call(
        _hgrn_in_kernel,
        grid=(NT_TOK // TM, 5),
        in_specs=[_tok_spec(D_MODEL), _norm_spec(0), _mod_spec(0), _mod_spec(1),
                  pl.BlockSpec((D_MODEL, D_MODEL), lambda i, j: (0, j))],
        out_specs=pl.BlockSpec((1, HGRN_HEADS, TM, HGRN_HD), lambda i, j: (j, 0, i, 0)),
        out_shape=jax.ShapeDtypeStruct((5, HGRN_HEADS, NT_TOK, HGRN_HD), BF16),
        scratch_shapes=[pltpu.VMEM((TM, D_MODEL), BF16)],
        compiler_params=_cparams(("parallel", "arbitrary")),
        name="hgrn_in",
    )(y, norm_l, mods_l, mods_l, w)


HGRN_NBLK = NT_TOK // HGRN_BLOCK
HGRN_PBLK = NP_TOK // HGRN_BLOCK
HGRN_SBLK = SAMPLE_LEN // HGRN_BLOCK
HGRN_NSEQ = N_PROMPT_SEQ + N_SAMPLE_SEQ


def _hgrn_seq_of_block(blk):
    return jnp.where(blk < HGRN_PBLK, blk, HGRN_PBLK + (blk - HGRN_PBLK) // HGRN_SBLK)


def _hgrn_scan_kernel(zq_ref, zf_ref, zv_ref, lb_ref, s0_ref, o_ref, sfin_ref, st_ref, *, rev):
    c = HGRN_CHUNK
    nsub = HGRN_BLOCK // c
    i = pl.program_id(0)
    blk = (HGRN_NBLK - 1 - i) if rev else i
    is_prompt = blk < HGRN_PBLK
    pos = (blk - HGRN_PBLK) % HGRN_SBLK
    seq_start = jnp.logical_or(is_prompt, pos == (HGRN_SBLK - 1 if rev else 0))
    seq_end = jnp.logical_or(is_prompt, pos == (0 if rev else HGRN_SBLK - 1))

    @pl.when(jnp.logical_and(seq_start, is_prompt))
    def _():
        st_ref[...] = jnp.zeros_like(st_ref)

    @pl.when(jnp.logical_and(seq_start, jnp.logical_not(is_prompt)))
    def _():
        def init(h, carry):
            st_ref[h] = s0_ref[0, h].T
            return carry
        lax.fori_loop(0, HGRN_HEADS, init, 0)

    row = lax.broadcasted_iota(jnp.int32, (c, HGRN_HD), 0)
    tt = lax.broadcasted_iota(jnp.int32, (c, c), 0)
    ss = lax.broadcasted_iota(jnp.int32, (c, c), 1)
    tri = ((ss >= tt) if rev else (ss <= tt)).astype(BF16)
    same_tile = (tt >> 3) == (ss >> 3)
    bands = [jnp.logical_and(same_tile, ss == ((tt + d) if rev else (tt - d))) for d in range(HGRN_DIAG)]
    levels = []
    size = 2 * HGRN_DIAG
    while size <= c:
        half = size // 2
        sh = size.bit_length() - 1
        row_hi = (row & (size - 1)) >= half
        t_hi = (tt & (size - 1)) >= half
        s_hi = (ss & (size - 1)) >= half
        if rev:
            is_query = jnp.logical_not(row_hi)
            pair = jnp.logical_and(jnp.logical_not(t_hi), s_hi)
        else:
            is_query = row_hi
            pair = jnp.logical_and(t_hi, jnp.logical_not(s_hi))
        pair = jnp.logical_and(pair, (tt >> sh) == (ss >> sh))
        levels.append((size, is_query, jnp.where(is_query, 1.0, -1.0), pair))
        size *= 2

    def shift1(x):
        x3 = x.reshape(c // HGRN_DIAG, HGRN_DIAG, HGRN_HD)
        return pltpu.roll(x3, (HGRN_DIAG - 1) if rev else 1, 1).reshape(c, HGRN_HD)

    def chunk_math(q, zf, v, lb, st):
        f = lb + (1.0 - lb) * jax.nn.sigmoid(zf)
        k = 1.0 - f
        g = jnp.log(f)
        g1 = g.astype(BF16)
        g2 = (g - g1.astype(F32)).astype(BF16)
        cum = _dot(tri, g1) + _dot(tri, g2)
        tot = cum[0:1, :] if rev else cum[c - 1:c, :]

        z = k
        att = jnp.where(bands[0], jnp.sum(q * z, axis=-1, keepdims=True), 0.0)
        for d in range(1, HGRN_DIAG):
            z = shift1(z) * f
            att = jnp.where(bands[d], jnp.sum(q * z, axis=-1, keepdims=True), att)

        for size, is_query, sign, pair in levels:
            half = size // 2
            pieces = []
            for b in range(c // size):
                ref_row = b * size + (half if rev else half - 1)
                pieces.append(jnp.broadcast_to(cum[ref_row:ref_row + 1, :], (size, HGRN_HD)))
            ref = pieces[0] if len(pieces) == 1 else jnp.concatenate(pieces, axis=0)
            x = (jnp.where(is_query, q, k) * jnp.exp((cum - ref) * sign)).astype(BF16)
            att = jnp.where(pair, _dot_nt(x, x), att)

        o = _dot_nt((q * jnp.exp(cum)).astype(BF16), st.astype(BF16)) + _dot(att.astype(BF16), v.astype(BF16))

        ke = (k * jnp.exp(tot - cum)).astype(BF16)
        st_new = jnp.exp(tot) * st + _dot(v.T.astype(BF16), ke)
        return o.astype(BF16), st_new

    ngrp = HGRN_HEADS // HGRN_UNROLL

    def body(step, carry):
        ci = step // ngrp
        ci = (nsub - 1 - ci) if rev else ci
        h0 = (step % ngrp) * HGRN_UNROLL
        r0 = pl.multiple_of(ci * c, c)
        rows = pl.ds(r0, c)
        ins = []
        for u in range(HGRN_UNROLL):
            h = h0 + u
            ins.append((zq_ref[0, h, rows, :].astype(F32), zf_ref[0, h, rows, :].astype(F32),
                        zv_ref[0, h, rows, :].astype(F32), lb_ref[h], st_ref[h]))
        outs = [chunk_math(*a) for a in ins]
        for u in range(HGRN_UNROLL):
            o_ref[h0 + u, rows, :] = outs[u][0]
            st_ref[h0 + u] = outs[u][1]
        return carry

    lax.fori_loop(0, nsub * ngrp, body, 0)

    @pl.when(seq_end)
    def _():
        def fin(h, carry):
            sfin_ref[0, h] = st_ref[h].T
            return carry
        lax.fori_loop(0, HGRN_HEADS, fin, 0)


def _hgrn_scan(p_hm, lb_d, s0_d, rev):
    def blk_of(i):
        return (HGRN_NBLK - 1 - i) if rev else i

    def sec_spec(sec):
        return pl.BlockSpec((1, HGRN_HEADS, HGRN_BLOCK, HGRN_HD), lambda i: (sec, 0, blk_of(i), 0))

    def seq_idx(i):
        return _hgrn_seq_of_block(blk_of(i))

    return pl.pallas_call(
        functools.partial(_hgrn_scan_kernel, rev=rev),
        grid=(HGRN_NBLK,),
        in_specs=[
            sec_spec(0), sec_spec(2 if rev else 1), sec_spec(3),
            pl.BlockSpec((HGRN_HEADS, 1, HGRN_HD), lambda i: (0, 0, 0)),
            pl.BlockSpec((1, HGRN_HEADS, HGRN_HD, HGRN_HD),
                         lambda i: (jnp.maximum(seq_idx(i) - N_PROMPT_SEQ, 0), 0, 0, 0)),
        ],
        out_specs=[
            pl.BlockSpec((HGRN_HEADS, HGRN_BLOCK, HGRN_HD), lambda i: (0, blk_of(i), 0)),
            pl.BlockSpec((1, HGRN_HEADS, HGRN_HD, HGRN_HD), lambda i: (seq_idx(i), 0, 0, 0)),
        ],
        out_shape=[
            jax.ShapeDtypeStruct((HGRN_HEADS, NT_TOK, HGRN_HD), BF16),
            jax.ShapeDtypeStruct((HGRN_NSEQ, HGRN_HEADS, HGRN_HD, HGRN_HD), F32),
        ],
        scratch_shapes=[pltpu.VMEM((HGRN_HEADS, HGRN_HD, HGRN_HD), F32)],
        compiler_params=_cparams(("arbitrary",)),
        name="hgrn_scan_bwd" if rev else "hgrn_scan_fwd",
    )(p_hm, p_hm, p_hm, lb_d, s0_d)


def _hgrn_out_kernel(of_ref, ob_ref, zg_ref, ng_ref, y_ref, g1_ref, gate_ref, w_ref, o_ref, x_ref):
    ng = ng_ref[...]
    for h in range(HGRN_HEADS):
        o = of_ref[h].astype(F32) + ob_ref[h].astype(F32)
        x_ref[:, h * HGRN_HD:(h + 1) * HGRN_HD] = (_rms(o, ng) * _silu(zg_ref[0, h].astype(F32))).astype(BF16)
    o_ref[...] = _post(y_ref, _dot(x_ref[...], w_ref[...]), g1_ref, gate_ref)


def _hgrn_out(o_f, o_b, p_hm, hgrn_norm_g, y, norm_l, mods_l, w):
    head_spec = pl.BlockSpec((HGRN_HEADS, TM, HGRN_HD), lambda i: (0, i, 0))
    return pl.pallas_call(
        _hgrn_out_kernel,
        grid=(NT_TOK // TM,),
        in_specs=[
            head_spec, head_spec,
            pl.BlockSpec((1, HGRN_HEADS, TM, HGRN_HD), lambda i: (4, 0, i, 0)),
            pl.BlockSpec((1, HGRN_HD), lambda i: (0, 0)),
            _tok_spec(D_MODEL), _norm_spec(1), _mod_spec(2),
            _resident((D_MODEL, D_MODEL)),
        ],
        out_specs=_tok_spec(D_MODEL),
        out_shape=jax.ShapeDtypeStruct((NT_TOK, D_MODEL), F32),
        scratch_shapes=[pltpu.VMEM((TM, D_MODEL), BF16)],
        compiler_params=_cparams(("parallel",)),
        name="hgrn_out",
    )(o_f, o_b, p_hm, hgrn_norm_g, y, norm_l, mods_l, w)


def _mla_down_kernel(y_ref, g0_ref, sh_ref, sc_ref, w_ref, qg_ref, kvg_ref, cos_ref, sin_ref,
                     cq_ref, ckv_ref, kpe_ref, kpe128_ref):
    p = _dot(_prenorm(y_ref, g0_ref, sh_ref, sc_ref), w_ref[...])
    cq_ref[...] = _rms(p[:, :MLA_Q_LORA], qg_ref[...]).astype(BF16)
    ckv_ref[...] = _rms(p[:, MLA_Q_LORA:MLA_Q_LORA + MLA_KV_LORA], kvg_ref[...])
    tail = p[:, MLA_Q_LORA + MLA_KV_LORA:]
    kpe_ref[...] = tail[:, :MLA_ROPE]
    kpe128_ref[...] = _rope128(tail, cos_ref[...], sin_ref[...]).astype(BF16)


def _mla_down(y, norm_l, mods_l, w, qg, kvg, cos_t, sin_t):
    return pl.pallas_call(
        _mla_down_kernel,
        grid=(NT_TOK // TM,),
        in_specs=[_tok_spec(D_MODEL), _norm_spec(0), _mod_spec(0), _mod_spec(1),
                  _resident((D_MODEL, MLA_DOWN_PAD)),
                  pl.BlockSpec((1, MLA_Q_LORA), lambda i: (0, 0)),
                  pl.BlockSpec((1, MLA_KV_LORA), lambda i: (0, 0)),
                  _rope_spec(), _rope_spec()],
        out_specs=[_tok_spec(MLA_Q_LORA), _tok_spec(MLA_KV_LORA), _tok_spec(MLA_ROPE), _tok_spec(128)],
        out_shape=[
            jax.ShapeDtypeStruct((NT_TOK, MLA_Q_LORA), BF16),
            jax.ShapeDtypeStruct((NT_TOK, MLA_KV_LORA), F32),
            jax.ShapeDtypeStruct((NT_TOK, MLA_ROPE), F32),
            jax.ShapeDtypeStruct((NT_TOK, 128), BF16),
        ],
        compiler_params=_cparams(("parallel",)),
        name="mla_down",
    )(y, norm_l, mods_l, mods_l, w, qg, kvg, cos_t, sin_t)


MLA_HB = 4


def _mla_q_kernel(cq_ref, w_ref, cos_ref, sin_ref, q_ref):
    acc = _dot(cq_ref[...], w_ref[0]) * MLA_Q_SCALE
    cos = cos_ref[...]
    sin = sin_ref[...]
    for hh in range(MLA_HB):
        c0 = hh * MLA_QK_PAD
        q_ref[hh, :, :MLA_NOPE] = acc[:, c0:c0 + MLA_NOPE].astype(BF16)
        q_ref[hh, :, MLA_NOPE:] = _rope128(acc[:, c0 + MLA_NOPE:c0 + MLA_QK_PAD], cos, sin).astype(BF16)


def _mla_q(cq, w_uq_g, cos_t, sin_t):
    return pl.pallas_call(
        _mla_q_kernel,
        grid=(NT_TOK // TM, MLA_HEADS // MLA_HB),
        in_specs=[_tok_spec(MLA_Q_LORA),
                  pl.BlockSpec((1, MLA_Q_LORA, MLA_HB * MLA_QK_PAD), lambda i, h: (h, 0, 0)),
                  _rope_spec(), _rope_spec()],
        out_specs=pl.BlockSpec((MLA_HB, TM, MLA_QK_PAD), lambda i, h: (h, i, 0)),
        out_shape=jax.ShapeDtypeStruct((MLA_HEADS, NT_TOK, MLA_QK_PAD), BF16),
        compiler_params=_cparams(("parallel", "arbitrary")),
        name="mla_q",
    )(cq, w_uq_g, cos_t, sin_t)


def _mla_kv_kernel(ckv_ref, kpe_ref, w_ref, k_ref, vt_ref):
    acc = _dot(ckv_ref[...], w_ref[0])
    kpe = kpe_ref[...]
    for hh in range(MLA_HB):
        c0 = hh * (MLA_NOPE + MLA_V)
        k_ref[hh, :, :MLA_NOPE] = acc[:, c0:c0 + MLA_NOPE].astype(BF16)
        k_ref[hh, :, MLA_NOPE:] = kpe
        vt_ref[hh] = acc[:, c0 + MLA_NOPE:c0 + MLA_NOPE + MLA_V].T.astype(BF16)


def _mla_kv(ckv_all, kpe_all, w_ukv_g):
    n = ckv_all.shape[0]
    return pl.pallas_call(
        _mla_kv_kernel,
        grid=(n // TM, MLA_HEADS // MLA_HB),
        in_specs=[_tok_spec(MLA_KV_LORA), _tok_spec(128),
                  pl.BlockSpec((1, MLA_KV_LORA, MLA_HB * (MLA_NOPE + MLA_V)), lambda i, h: (h, 0, 0))],
        out_specs=[pl.BlockSpec((MLA_HB, TM, MLA_QK_PAD), lambda i, h: (h, i, 0)),
                   pl.BlockSpec((MLA_HB, MLA_V, TM), lambda i, h: (h, 0, i))],
        out_shape=[jax.ShapeDtypeStruct((MLA_HEADS, n, MLA_QK_PAD), BF16),
                   jax.ShapeDtypeStruct((MLA_HEADS, MLA_V, n), BF16)],
        compiler_params=_cparams(("parallel", "arbitrary")),
        name="mla_kv",
    )(ckv_all, kpe_all, w_ukv_g)


def _row_partial(x, reduce_fn, combine_fn):
    rows, n = x.shape
    x3 = x.reshape(rows // 8, 8, n)
    parts = min(ROW_REDUCE_PARTS, rows // 8)
    bounds = [i * (rows // 8) // parts for i in range(parts + 1)]
    acc = [reduce_fn(x3[lo:hi], axis=0) for lo, hi in zip(bounds[:-1], bounds[1:])]
    while len(acc) > 1:
        acc = [combine_fn(a, b) for a, b in zip(acc[0::2], acc[1::2])] + ([acc[-1]] if len(acc) % 2 else [])
    return acc[0]


def _mla_attn_kernel(q_ref, k_ref, vt_ref, *rest, bounds):
    o_ref = rest[-1]
    q = q_ref[0]
    m = den = acc = None
    for lo, hi in zip(bounds[:-1], bounds[1:]):
        s = _dot_nt(k_ref[0, lo:hi, :], q)
        mc = jnp.max(_row_partial(s, jnp.max, jnp.maximum), axis=0, keepdims=True)
        if m is None:
            m = mc
            p = jnp.exp2(s - m)
            den = jnp.sum(_row_partial(p, jnp.sum, jnp.add), axis=0, keepdims=True)
            acc = _dot(vt_ref[0, :, lo:hi], p.astype(BF16))
        else:
            m_new = jnp.maximum(m, mc)
            alpha = jnp.exp2(m - m_new)
            p = jnp.exp2(s - m_new)
            den = alpha * den + jnp.sum(_row_partial(p, jnp.sum, jnp.add), axis=0, keepdims=True)
            acc = alpha * acc + _dot(vt_ref[0, :, lo:hi], p.astype(BF16))
            m = m_new
    o_ref[...] = (acc / den).T.astype(BF16)


MLA_TQ = 512
ROW_REDUCE_PARTS = 4


def _mla_attn_pipe_kernel(q_ref, k_ref, vt_ref, o_ref, s_ref):
    tq = MLA_TQ
    nt = SAMPLE_LEN // tq
    chunks = list(zip(MLA_KEY_CHUNKS[:-1], MLA_KEY_CHUNKS[1:]))

    def tile_rows(t):
        return pl.ds(t * tq, tq) if isinstance(t, int) else pl.ds(pl.multiple_of(t * tq, tq), tq)

    def stage(t_a, slot_a, t_b, slot_b, m_b):
        if t_a is not None:
            q = q_ref[0, tile_rows(t_a), :]
        m_parts = []
        den = acc = None
        for lo, hi in chunks:
            if t_a is not None:
                s = _dot_nt(k_ref[0, lo:hi, :], q)
                s_ref[slot_a, lo:hi, :] = s
                m_parts.append(_row_partial(s, jnp.max, jnp.maximum))
            if t_b is not None:
                p = jnp.exp2(s_ref[slot_b, lo:hi, :] - m_b)
                d = _row_partial(p, jnp.sum, jnp.add)
                a = _dot(vt_ref[0, :, lo:hi], p.astype(BF16))
                den = d if den is None else den + d
                acc = a if acc is None else acc + a
        if t_b is not None:
            den = jnp.sum(den, axis=0, keepdims=True)
            o_ref[tile_rows(t_b), :] = (acc / den).T.astype(BF16)
        if t_a is None:
            return None
        m_a = m_parts[0]
        for part in m_parts[1:]:
            m_a = jnp.maximum(m_a, part)
        return jnp.max(m_a, axis=0, keepdims=True)

    m = stage(0, 0, None, None, None)

    def body(j, m_prev):
        t = 2 * j + 1
        m_mid = stage(t, 1, t - 1, 0, m_prev)
        return stage(t + 1, 0, t, 1, m_mid)

    m = lax.fori_loop(0, (nt - 2) // 2, body, m)
    m = stage(nt - 1, 1, nt - 2, 0, m)
    stage(None, None, nt - 1, 1, m)


def _mla_attn(q_hm, k_hm, vt_hm):
    pk0 = N_SAMPLE_SEQ * MLA_KV_LEN // PROMPT_LEN
    o_p = pl.pallas_call(
        functools.partial(_mla_attn_kernel, bounds=(0, PROMPT_LEN)),
        grid=(MLA_HEADS, N_PROMPT_SEQ),
        in_specs=[pl.BlockSpec((1, PROMPT_LEN, MLA_QK_PAD), lambda h, b: (h, b, 0)),
                  pl.BlockSpec((1, PROMPT_LEN, MLA_QK_PAD), lambda h, b: (h, pk0 + b, 0)),
                  pl.BlockSpec((1, MLA_V, PROMPT_LEN), lambda h, b: (h, 0, pk0 + b))],
        out_specs=pl.BlockSpec((PROMPT_LEN, MLA_V), lambda h, b: (b, h)),
        out_shape=jax.ShapeDtypeStruct((NP_TOK, MLA_HEADS * MLA_V), BF16),
        compiler_params=_cparams(("parallel", "parallel")),
        name="mla_attn_prompt",
    )(q_hm, k_hm, vt_hm)
    q0 = NP_TOK // SAMPLE_LEN
    o_s = pl.pallas_call(
        _mla_attn_pipe_kernel,
        grid=(MLA_HEADS, N_SAMPLE_SEQ),
        in_specs=[pl.BlockSpec((1, SAMPLE_LEN, MLA_QK_PAD), lambda h, b: (h, q0 + b, 0)),
                  pl.BlockSpec((1, MLA_KV_LEN, MLA_QK_PAD), lambda h, b: (h, b, 0)),
                  pl.BlockSpec((1, MLA_V, MLA_KV_LEN), lambda h, b: (h, 0, b))],
        out_specs=pl.BlockSpec((SAMPLE_LEN, MLA_V), lambda h, b: (b, h)),
        out_shape=jax.ShapeDtypeStruct((NS_TOK, MLA_HEADS * MLA_V), BF16),
        scratch_shapes=[pltpu.VMEM((2, MLA_KV_LEN, MLA_TQ), F32)],
        compiler_params=_cparams(("parallel", "parallel")),
        name="mla_attn_sample",
    )(q_hm, k_hm, vt_hm)
    return o_p, o_s


def _swa_qkv_kernel(y_ref, g0_ref, sh_ref, sc_ref, w_ref, cos_ref, sin_ref, q_ref, kv_ref, kf_ref, vf_ref):
    p = _dot(_prenorm(y_ref, g0_ref, sh_ref, sc_ref), w_ref[...])
    cos = cos_ref[...]
    sin = sin_ref[...]
    kf_ref[...] = p[:, SWA_QW:SWA_QW + SWA_KVW]
    vf_ref[...] = p[:, SWA_QW + SWA_KVW:]
    for c in range(SWA_QW // 128):
        r = _rope128(p[:, c * 128:(c + 1) * 128], cos, sin).astype(BF16)
        q_ref[2 * c] = r[:, :SWA_HD]
        q_ref[2 * c + 1] = r[:, SWA_HD:]
    for c in range(SWA_KVW // 128):
        k0 = SWA_QW + c * 128
        v0 = SWA_QW + SWA_KVW + c * 128
        k = _rope128(p[:, k0:k0 + 128], cos, sin).astype(BF16)
        v = p[:, v0:v0 + 128].astype(BF16)
        kv_ref[2 * c] = jnp.concatenate([k[:, :SWA_HD], v[:, :SWA_HD]], axis=1)
        kv_ref[2 * c + 1] = jnp.concatenate([k[:, SWA_HD:], v[:, SWA_HD:]], axis=1)


def _swa_qkv(y, norm_l, mods_l, w, cos_t, sin_t):
    return pl.pallas_call(
        _swa_qkv_kernel,
        grid=(NT_TOK // TM,),
        in_specs=[_tok_spec(D_MODEL), _norm_spec(0), _mod_spec(0), _mod_spec(1),
                  _resident((D_MODEL, SWA_QW + 2 * SWA_KVW)), _rope_spec(), _rope_spec()],
        out_specs=[pl.BlockSpec((SWA_Q_HEADS, TM, SWA_HD), lambda i: (0, i, 0)),
                   pl.BlockSpec((SWA_KV_HEADS, TM, 2 * SWA_HD), lambda i: (0, i, 0)),
                   _tok_spec(SWA_KVW), _tok_spec(SWA_KVW)],
        out_shape=[
            jax.ShapeDtypeStruct((SWA_Q_HEADS, NT_TOK, SWA_HD), BF16),
            jax.ShapeDtypeStruct((SWA_KV_HEADS, NT_TOK, 2 * SWA_HD), BF16),
            jax.ShapeDtypeStruct((NT_TOK, SWA_KVW), F32),
            jax.ShapeDtypeStruct((NT_TOK, SWA_KVW), F32),
        ],
        compiler_params=_cparams(("parallel",)),
        name="swa_qkv",
    )(y, norm_l, mods_l, mods_l, w, cos_t, sin_t)


def _swa_attn_kernel(sink_ref, q_ref, kvc_ref, *rest, local):
    o_ref = rest[-1]
    kv0 = pl.program_id(0) * SWA_KVB
    tq = q_ref.shape[1]
    nq = SWA_GROUP * tq
    kvc = kvc_ref[...].reshape(SWA_KVB, kvc_ref.shape[-2], 2 * SWA_HD)
    head = lax.broadcasted_iota(jnp.int32, (1, nq), 1) >> (tq.bit_length() - 1)
    if local:
        j = pl.program_id(2)
        nb = pl.num_programs(2)
        key = lax.broadcasted_iota(jnp.int32, (SWA_BLOCK, nq), 0)
        qry = lax.broadcasted_iota(jnp.int32, (SWA_BLOCK, nq), 1) & (SWA_BLOCK - 1)
        keep = [jnp.logical_and(key >= qry, j > 0), None, jnp.logical_and(key <= qry, j < nb - 1)]
    for u in range(SWA_KVB):
        q = q_ref[u * SWA_GROUP:(u + 1) * SWA_GROUP].reshape(nq, SWA_HD) * (SWA_HD ** -0.5)
        sink = jnp.zeros((1, nq), F32)
        for g in range(SWA_GROUP):
            sink = jnp.where(head == g, sink_ref[(kv0 + u) * SWA_GROUP + g], sink)
        blocks = [kvc[u]]
        scores = [_dot_nt(blocks[0][:, :SWA_HD], q)]
        if local:
            for t in range(3):
                blk = rest[t][u]
                s = _dot_nt(blk[:, :SWA_HD], q)
                scores.append(s if keep[t] is None else jnp.where(keep[t], s, NEG_INF))
                blocks.append(blk)
        s = scores[0] if len(scores) == 1 else jnp.concatenate(scores, axis=0)
        kv_all = blocks[0] if len(blocks) == 1 else jnp.concatenate(blocks, axis=0)
        m = jnp.maximum(jnp.max(_row_partial(s, jnp.max, jnp.maximum), axis=0, keepdims=True), sink)
        p = jnp.exp(s - m)
        den = jnp.sum(_row_partial(p, jnp.sum, jnp.add), axis=0, keepdims=True) + jnp.exp(sink - m)
        kv_t = kv_all.astype(F32).T.astype(BF16)
        out = (_dot(kv_t, p.astype(BF16)) / den).T.astype(BF16)
        for g in range(SWA_GROUP):
            c0 = (u * SWA_GROUP + g) * SWA_HD
            o_ref[:, c0:c0 + SWA_HD] = out[g * tq:(g + 1) * tq, SWA_HD:]


def _swa_attn(sink, q_hm, kv_hm, kvc_s):
    smem = pl.BlockSpec(memory_space=pltpu.SMEM)
    gw = SWA_KVB * SWA_GROUP * SWA_HD
    qb = SWA_KVB * SWA_GROUP
    o_p = pl.pallas_call(
        functools.partial(_swa_attn_kernel, local=False),
        grid=(SWA_KV_HEADS // SWA_KVB, N_PROMPT_SEQ),
        in_specs=[smem,
                  pl.BlockSpec((qb, PROMPT_LEN, SWA_HD), lambda kv, b: (kv, b, 0)),
                  pl.BlockSpec((SWA_KVB, PROMPT_LEN, 2 * SWA_HD), lambda kv, b: (kv, b, 0))],
        out_specs=pl.BlockSpec((PROMPT_LEN, gw), lambda kv, b: (b, kv)),
        out_shape=jax.ShapeDtypeStruct((NP_TOK, SWA_QW), BF16),
        compiler_params=_cparams(("parallel", "parallel")),
        name="swa_attn_prompt",
    )(sink, q_hm, kv_hm)
    nb = SAMPLE_LEN // SWA_BLOCK
    b0 = NP_TOK // SWA_BLOCK

    def loc(off):
        return pl.BlockSpec((SWA_KVB, SWA_BLOCK, 2 * SWA_HD),
                            lambda kv, b, j: (kv, b0 + b * nb + jnp.clip(j + off, 0, nb - 1), 0))

    o_s = pl.pallas_call(
        functools.partial(_swa_attn_kernel, local=True),
        grid=(SWA_KV_HEADS // SWA_KVB, N_SAMPLE_SEQ, nb),
        in_specs=[smem,
                  pl.BlockSpec((qb, SWA_BLOCK, SWA_HD), lambda kv, b, j: (kv, b0 + b * nb + j, 0)),
                  pl.BlockSpec((1, SWA_KVB, PAST_LEN, 2 * SWA_HD), lambda kv, b, j: (b, kv, 0, 0)),
                  loc(-1), loc(0), loc(1)],
        out_specs=pl.BlockSpec((SWA_BLOCK, gw), lambda kv, b, j: (b * nb + j, kv)),
        out_shape=jax.ShapeDtypeStruct((NS_TOK, SWA_QW), BF16),
        compiler_params=_cparams(("parallel", "parallel", "arbitrary")),
        name="swa_attn_sample",
    )(sink, q_hm, kvc_s, kv_hm, kv_hm, kv_hm)
    return o_p, o_s


def _rope_tables():
    t = jnp.arange(SAMPLE_LEN, dtype=jnp.int32)
    inv = ROPE_BASE ** (-jnp.arange(16, dtype=F32) / 16)
    ang_r = (t // GRID_W).astype(F32)[:, None] * inv[None, :]
    ang_c = (t % GRID_W).astype(F32)[:, None] * inv[None, :]
    cos = jnp.concatenate([jnp.cos(ang_r), jnp.cos(ang_r), jnp.cos(ang_c), jnp.cos(ang_c)], axis=-1)
    sin = jnp.concatenate([-jnp.sin(ang_r), jnp.sin(ang_r), -jnp.sin(ang_c), jnp.sin(ang_c)], axis=-1)
    cos = jnp.concatenate([cos, cos], axis=-1)
    sin = jnp.concatenate([sin, sin], axis=-1)
    cos_t = jnp.concatenate([jnp.ones((NP_TOK, 128), F32), cos], axis=0)
    sin_t = jnp.concatenate([jnp.zeros((NP_TOK, 128), F32), sin], axis=0)
    return cos_t, sin_t


def kernel(x_prompt, x_sample, c, state_hgrn, cache_mla_ckv, cache_mla_kpe, cache_swa_k, cache_swa_v, c_ctx, ada_w, ada_b, norm_g, mlp_w_in, mlp_w_out, hgrn_w_in, hgrn_lb_logits, hgrn_norm_g, hgrn_w_out, mla_w_down, mla_q_norm_g, mla_kv_norm_g, mla_w_uq, mla_w_ukv, mla_w_out, swa_w_qkv, swa_sink, swa_w_out):
    y = jnp.concatenate([x_prompt.reshape(NP_TOK, D_MODEL), x_sample.reshape(NS_TOK, D_MODEL)], axis=0)
    cond8 = jnp.concatenate([c_ctx[None, :], c, jnp.zeros((3, D_MODEL), F32)], axis=0)
    mods = _adaln(cond8, ada_w, ada_b).reshape(DEPTH, 8, N_MOD, 1, D_MODEL)
    cos_t, sin_t = _rope_tables()

    lb_soft = jax.nn.softmax(hgrn_lb_logits.astype(F32), axis=1)
    lb_all = jnp.cumsum(lb_soft, axis=1) - lb_soft[:, :1]

    new_hgrn = []
    new_ckv = new_kpe = new_k = new_v = None
    for layer in range(DEPTH):
        kind, j = layer % 3, layer // 3
        norm_l = norm_g[layer].reshape(4, 1, D_MODEL)
        mods_l = mods[layer]
        if kind == 0:
            p_hm = _hgrn_in(y, norm_l, mods_l, hgrn_w_in[j].astype(BF16))
            lb = lb_all[:, layer].reshape(2, HGRN_HEADS, 1, HGRN_HD)
            o_f, s_f = _hgrn_scan(p_hm, lb[0], state_hgrn[:, j, 0], rev=False)
            o_b, s_b = _hgrn_scan(p_hm, lb[1], state_hgrn[:, j, 1], rev=True)
            new_hgrn.append(jnp.stack([s_f[:N_PROMPT_SEQ], s_b[:N_PROMPT_SEQ]], axis=1))
            y = _hgrn_out(o_f, o_b, p_hm, hgrn_norm_g[j].reshape(1, HGRN_HD), y, norm_l, mods_l,
                          hgrn_w_out[j].astype(BF16))
        elif kind == 1:
            w_down = jnp.pad(mla_w_down[j], ((0, 0), (0, MLA_DOWN_PAD - mla_w_down.shape[-1]))).astype(BF16)
            cq, ckv, kpe, kpe128 = _mla_down(y, norm_l, mods_l, w_down, mla_q_norm_g[j].reshape(1, -1),
                                             mla_kv_norm_g[j].reshape(1, -1), cos_t, sin_t)
            ng = MLA_HEADS // MLA_HB
            w_uq = jnp.pad(mla_w_uq[j].reshape(MLA_Q_LORA, MLA_HEADS, MLA_QK),
                           ((0, 0), (0, 0), (0, MLA_QK_PAD - MLA_QK)))
            w_uq = w_uq.reshape(MLA_Q_LORA, ng, MLA_HB * MLA_QK_PAD).transpose(1, 0, 2).astype(BF16)
            w_ukv = mla_w_ukv[j].reshape(MLA_KV_LORA, ng, MLA_HB * (MLA_NOPE + MLA_V)).transpose(1, 0, 2).astype(BF16)
            q_hm = _mla_q(cq, w_uq, cos_t, sin_t)
            ckv_s = jnp.concatenate([cache_mla_ckv[:, j].astype(BF16),
                                     ckv[NP_TOK:].reshape(N_SAMPLE_SEQ, SAMPLE_LEN, MLA_KV_LORA).astype(BF16)], axis=1)
            kpe_ctx = jnp.pad(cache_mla_kpe[:, j], ((0, 0), (0, 0), (0, 128 - MLA_ROPE))).astype(BF16)
            kpe_s = jnp.concatenate([kpe_ctx, kpe128[NP_TOK:].reshape(N_SAMPLE_SEQ, SAMPLE_LEN, 128)], axis=1)
            ckv_all = jnp.concatenate([ckv_s.reshape(-1, MLA_KV_LORA), ckv[:NP_TOK].astype(BF16)], axis=0)
            kpe_all = jnp.concatenate([kpe_s.reshape(-1, 128), kpe128[:NP_TOK]], axis=0)
            k_hm, v_hm = _mla_kv(ckv_all, kpe_all, w_ukv)
            o_p, o_s = _mla_attn(q_hm, k_hm, v_hm)
            new_ckv = ckv[:NP_TOK].reshape(N_PROMPT_SEQ, 1, PROMPT_LEN, MLA_KV_LORA)
            new_kpe = kpe[:NP_TOK].reshape(N_PROMPT_SEQ, 1, PROMPT_LEN, MLA_ROPE)
            y = _outproj(o_p, o_s, y, norm_l, mods_l, mla_w_out[j].astype(BF16))
        else:
            q_hm, kv_hm, kf, vf = _swa_qkv(y, norm_l, mods_l, swa_w_qkv[j].astype(BF16), cos_t, sin_t)
            kvc_s = jnp.concatenate([cache_swa_k[:, j], cache_swa_v[:, j]], axis=-1).transpose(0, 2, 1, 3).astype(BF16)
            o_p, o_s = _swa_attn(swa_sink[j], q_hm, kv_hm, kvc_s)
            new_k = kf[:NP_TOK].reshape(N_PROMPT_SEQ, 1, PROMPT_LEN, SWA_KV_HEADS, SWA_HD)
            new_v = vf[:NP_TOK].reshape(N_PROMPT_SEQ, 1, PROMPT_LEN, SWA_KV_HEADS, SWA_HD)
            y = _outproj(o_p, o_s, y, norm_l, mods_l, swa_w_out[j].astype(BF16))
        w1 = mlp_w_in[layer].astype(BF16)
        w2 = mlp_w_out[layer].astype(BF16)
        if layer < DEPTH - 1:
            y = _mlp(y, norm_l, mods_l, w1, w2, 0, NT_TOK // TM)
        else:
            y_prompt = _mlp(y, norm_l, mods_l, w1, w2, 0, PROMPT_TILES)
            y_sample = _mlp(y, norm_l, mods_l, w1, w2, PROMPT_TILES, NS_TOK // TM)

    return (y_prompt.reshape(N_PROMPT_SEQ, PROMPT_LEN, D_MODEL), y_sample.reshape(N_SAMPLE_SEQ, SAMPLE_LEN, D_MODEL),
            jnp.stack(new_hgrn, axis=1), new_ckv, new_kpe, new_k, new_v)
```

```python
import functools

import jax
import jax.numpy as jnp
from jax import lax
from jax.experimental import pallas as pl
from jax.experimental.pallas import tpu as pltpu

F32 = jnp.float32
BF16 = jnp.bfloat16

D_MODEL = 2048
DEPTH = 4
N_PROMPT_SEQ = 16
PROMPT_LEN = 256
N_SAMPLE_SEQ = 4
SAMPLE_LEN = 4096
PAST_LEN = 256
NP_TOK = N_PROMPT_SEQ * PROMPT_LEN
NS_TOK = N_SAMPLE_SEQ * SAMPLE_LEN
NT_TOK = NP_TOK + NS_TOK
GROUP_TOK = 4096
N_GROUPS = NT_TOK // GROUP_TOK
N_MOD = 6
D_FF = 4 * D_MODEL
NORM_EPS = 1e-6
GRID_W = 64
ROPE_BASE = 10000.0

HGRN_HEADS = 16
HGRN_HD = 128
HGRN_CHUNK = 128
HGRN_BLOCK = 256
HGRN_DIAG = 8
HGRN_UNROLL = 8

MLA_HEADS = 16
MLA_Q_LORA = 512
MLA_KV_LORA = 512
MLA_NOPE = 128
MLA_ROPE = 64
MLA_V = 128
MLA_QK = MLA_NOPE + MLA_ROPE
MLA_QK_PAD = 256
MLA_DOWN_PAD = 1152
MLA_KV_LEN = PAST_LEN + SAMPLE_LEN
MLA_KEY_CHUNKS = (0, 640, 1280, 1792, 2304, 2816, 3328, 3840, MLA_KV_LEN)
MLA_Q_SCALE = MLA_QK ** -0.5 * 1.4426950408889634

SWA_HD = 64
SWA_Q_HEADS = 32
SWA_KV_HEADS = 8
SWA_GROUP = SWA_Q_HEADS // SWA_KV_HEADS
SWA_QW = SWA_Q_HEADS * SWA_HD
SWA_KVW = SWA_KV_HEADS * SWA_HD
SWA_BLOCK = 128
SWA_KVB = 8
NEG_INF = -1e30

TM = 512
TILES_PER_GROUP = GROUP_TOK // TM
VMEM_LIMIT = 56 * 1024 * 1024


def _cparams(sem):
    return pltpu.CompilerParams(dimension_semantics=sem, vmem_limit_bytes=VMEM_LIMIT)


def _dot(a, b):
    return jnp.dot(a, b, preferred_element_type=F32)


def _dot_nt(a, b):
    return lax.dot_general(a, b, (((1,), (1,)), ((), ())), preferred_element_type=F32)


def _rms(x, g):
    return x * lax.rsqrt(jnp.mean(x * x, axis=-1, keepdims=True) + NORM_EPS) * g


def _silu(x):
    return x * jax.nn.sigmoid(x)


def _prenorm(y_ref, g_ref, sh_ref, sc_ref):
    return (_rms(y_ref[...], g_ref[0]) * (1.0 + sc_ref[0, 0]) + sh_ref[0, 0]).astype(BF16)


def _post(y_ref, acc, g_ref, gate_ref):
    return y_ref[...] + gate_ref[0, 0] * _rms(acc, g_ref[0])


def _rope128(x, cos, sin):
    lane = lax.broadcasted_iota(jnp.int32, x.shape, 1)
    first = (lane & 31) < 16
    swapped = jnp.where(first, pltpu.roll(x, 112, 1), pltpu.roll(x, 16, 1))
    return x * cos + swapped * sin


def _tok_spec(n):
    return pl.BlockSpec((TM, n), lambda i, *_: (i, 0))


def _norm_spec(k):
    return pl.BlockSpec((1, 1, D_MODEL), lambda i, *_: (k, 0, 0))


def _mod_spec(k):
    return pl.BlockSpec((1, 1, 1, D_MODEL), lambda i, *_: (i // TILES_PER_GROUP, k, 0, 0))


def _rope_spec():
    def idx(i, *_):
        return (jnp.where(i < TILES_PER_GROUP, i, TILES_PER_GROUP + i % TILES_PER_GROUP), 0)
    return pl.BlockSpec((TM, 128), idx)


def _resident(shape):
    nd = len(shape)
    return pl.BlockSpec(shape, lambda *_: (0,) * nd, pipeline_mode=pl.Buffered(1))


def _adaln_kernel(c_ref, w_ref, b_ref, o_ref):
    s = _silu(c_ref[...])
    s_hi = s.astype(BF16)
    s_lo = (s - s_hi.astype(F32)).astype(BF16)
    w = w_ref[0]
    w_hi = w.astype(BF16)
    w_lo = (w - w_hi.astype(F32)).astype(BF16)
    o_ref[0] = _dot(s_hi, w_hi) + _dot(s_lo, w_hi) + _dot(s_hi, w_lo) + b_ref[0]


def _adaln(cond8, ada_w, ada_b):
    tn = 1024
    n = N_MOD * D_MODEL
    return pl.pallas_call(
        _adaln_kernel,
        grid=(DEPTH, n // tn),
        in_specs=[
            pl.BlockSpec((8, D_MODEL), lambda l, j: (0, 0)),
            pl.BlockSpec((1, D_MODEL, tn), lambda l, j: (l, 0, j)),
            pl.BlockSpec((1, 1, tn), lambda l, j: (l, 0, j)),
        ],
        out_specs=pl.BlockSpec((1, 8, tn), lambda l, j: (l, 0, j)),
        out_shape=jax.ShapeDtypeStruct((DEPTH, 8, n), F32),
        compiler_params=_cparams(("parallel", "parallel")),
        name="adaln",
    )(cond8, ada_w, ada_b.reshape(DEPTH, 1, n))


def _prenorm_rows(y_ref, rows, g_ref, sh_ref, sc_ref):
    return (_rms(y_ref[rows, :], g_ref[0]) * (1.0 + sc_ref[0, 0]) + sh_ref[0, 0]).astype(BF16)


def _mlp_kernel(y_ref, yn_ref, g2_ref, g3_ref, sh_ref, sc_ref, shn_ref, scn_ref, gate_ref, w1_ref, w2_ref,
                o_ref, h_ref, *, nf):
    i = pl.program_id(0)
    f = pl.program_id(1)
    cur = i % 2
    all_rows = pl.ds(0, TM)

    @pl.when(jnp.logical_and(i == 0, f == 0))
    def _():
        h_ref[0] = _prenorm_rows(y_ref, all_rows, g2_ref, sh_ref, sc_ref)

    def partial_sum():
        u = jnp.maximum(_dot(h_ref[cur], w1_ref[...]), 0.0)
        part = _dot((u * u).astype(BF16), w2_ref[...])
        rows = pl.ds(pl.multiple_of(f * (TM // nf), TM // nf), TM // nf)
        h_ref[1 - cur, rows, :] = _prenorm_rows(yn_ref, rows, g2_ref, shn_ref, scn_ref)
        return part

    @pl.when(f == 0)
    def _():
        o_ref[...] = partial_sum()

    @pl.when(jnp.logical_and(f > 0, f < nf - 1))
    def _():
        o_ref[...] += partial_sum()

    @pl.when(f == nf - 1)
    def _():
        o_ref[...] = _post(y_ref, o_ref[...] + partial_sum(), g3_ref, gate_ref)


def _mlp(y, norm_l, mods_l, w1, w2, tile0, ntiles):
    tf = 1024

    def nxt(i):
        return jnp.minimum(i + 1, ntiles - 1) + tile0

    def mod(k, tile_of):
        return pl.BlockSpec((1, 1, 1, D_MODEL), lambda i, f: (tile_of(i) // TILES_PER_GROUP, k, 0, 0))

    def cur(i):
        return i + tile0

    return pl.pallas_call(
        functools.partial(_mlp_kernel, nf=D_FF // tf),
        grid=(ntiles, D_FF // tf),
        in_specs=[
            pl.BlockSpec((TM, D_MODEL), lambda i, f: (cur(i), 0)),
            pl.BlockSpec((TM, D_MODEL), lambda i, f: (nxt(i), 0)),
            _norm_spec(2), _norm_spec(3),
            mod(3, cur), mod(4, cur), mod(3, nxt), mod(4, nxt), mod(5, cur),
            pl.BlockSpec((D_MODEL, tf), lambda i, f: (0, f)),
            pl.BlockSpec((tf, D_MODEL), lambda i, f: (f, 0)),
        ],
        out_specs=_tok_spec(D_MODEL),
        out_shape=jax.ShapeDtypeStruct((ntiles * TM, D_MODEL), F32),
        scratch_shapes=[pltpu.VMEM((2, TM, D_MODEL), BF16)],
        compiler_params=_cparams(("arbitrary", "arbitrary")),
        name="mlp",
    )(y, y, norm_l, norm_l, mods_l, mods_l, mods_l, mods_l, mods_l, w1, w2)


PROMPT_TILES = NP_TOK // TM


def _outproj_kernel(xp_ref, xs_ref, y_ref, g1_ref, gate_ref, w_ref, o_ref):
    x = jnp.where(pl.program_id(0) < PROMPT_TILES, xp_ref[...], xs_ref[...])
    o_ref[...] = _post(y_ref, _dot(x, w_ref[...]), g1_ref, gate_ref)


def _outproj(x_p, x_s, y, norm_l, mods_l, w):
    return pl.pallas_call(
        _outproj_kernel,
        grid=(NT_TOK // TM,),
        in_specs=[pl.BlockSpec((TM, D_MODEL), lambda i: (jnp.minimum(i, PROMPT_TILES - 1), 0)),
                  pl.BlockSpec((TM, D_MODEL), lambda i: (jnp.maximum(i - PROMPT_TILES, 0), 0)),
                  _tok_spec(D_MODEL), _norm_spec(1), _mod_spec(2),
                  _resident((D_MODEL, D_MODEL))],
        out_specs=_tok_spec(D_MODEL),
        out_shape=jax.ShapeDtypeStruct((NT_TOK, D_MODEL), F32),
        compiler_params=_cparams(("parallel",)),
        name="outproj",
    )(x_p, x_s, y, norm_l, mods_l, w)


HGRN_IN_SLICE = 128


def _hgrn_in_kernel(y_ref, yn_ref, g0_ref, sh_ref, sc_ref, shn_ref, scn_ref, w_ref, o_ref, ha_ref, hb_ref):
    i = pl.program_id(0)
    j = pl.program_id(1)

    @pl.when(jnp.logical_and(i == 0, j == 0))
    def _():
        ha_ref[...] = _prenorm_rows(y_ref, pl.ds(0, TM), g0_ref, sh_ref, sc_ref)

    def step(h_cur, h_next):
        s = jnp.minimum(j, TM // HGRN_IN_SLICE - 1)
        rows = pl.ds(pl.multiple_of(s * HGRN_IN_SLICE, HGRN_IN_SLICE), HGRN_IN_SLICE)
        h_next[rows, :] = _prenorm_rows(yn_ref, rows, g0_ref, shn_ref, scn_ref)
        acc = _dot(h_cur[...], w_ref[...])
        for c in range(HGRN_HEADS):
            o_ref[0, c] = acc[:, c * HGRN_HD:(c + 1) * HGRN_HD].astype(BF16)

    @pl.when(i % 2 == 0)
    def _():
        step(ha_ref, hb_ref)

    @pl.when(i % 2 == 1)
    def _():
        step(hb_ref, ha_ref)


def _hgrn_in(y, norm_l, mods_l, w):
    nt = NT_TOK // TM

    def nxt(i):
        return jnp.minimum(i + 1, nt - 1)

    def mod_next(k):
        return pl.BlockSpec((1, 1, 1, D_MODEL), lambda i, j: (nxt(i) // TILES_PER_GROUP, k, 0, 0))

    return pl.pallas_call(
        _hgrn_in_kernel,
        grid=(nt, 5),
        in_specs=[_tok_spec(D_MODEL), pl.BlockSpec((TM, D_MODEL), lambda i, j: (nxt(i), 0)),
                  _norm_spec(0), _mod_spec(0), _mod_spec(1), mod_next(0), mod_next(1),
                  pl.BlockSpec((D_MODEL, D_MODEL), lambda i, j: (0, j))],
        out_specs=pl.BlockSpec((1, HGRN_HEADS, TM, HGRN_HD), lambda i, j: (j, 0, i, 0)),
        out_shape=jax.ShapeDtypeStruct((5, HGRN_HEADS, NT_TOK, HGRN_HD), BF16),
        scratch_shapes=[pltpu.VMEM((TM, D_MODEL), BF16), pltpu.VMEM((TM, D_MODEL), BF16)],
        compiler_params=_cparams(("arbitrary", "arbitrary")),
        name="hgrn_in",
    )(y, y, norm_l, mods_l, mods_l, mods_l, mods_l, w)


HGRN_NBLK = NT_TOK // HGRN_BLOCK
HGRN_PBLK = NP_TOK // HGRN_BLOCK
HGRN_SBLK = SAMPLE_LEN // HGRN_BLOCK
HGRN_NSEQ = N_PROMPT_SEQ + N_SAMPLE_SEQ


def _hgrn_seq_of_block(blk):
    return jnp.where(blk < HGRN_PBLK, blk, HGRN_PBLK + (blk - HGRN_PBLK) // HGRN_SBLK)


def _hgrn_scan_kernel(zq_ref, zf_ref, zv_ref, lb_ref, s0_ref, o_ref, sfin_ref, st_ref, *, rev):
    c = HGRN_CHUNK
    nsub = HGRN_BLOCK // c
    i = pl.program_id(0)
    blk = (HGRN_NBLK - 1 - i) if rev else i
    is_prompt = blk < HGRN_PBLK
    pos = (blk - HGRN_PBLK) % HGRN_SBLK
    seq_start = jnp.logical_or(is_prompt, pos == (HGRN_SBLK - 1 if rev else 0))

    @pl.when(jnp.logical_and(seq_start, is_prompt))
    def _():
        st_ref[...] = jnp.zeros_like(st_ref)

    @pl.when(jnp.logical_and(seq_start, jnp.logical_not(is_prompt)))
    def _():
        def init(h, carry):
            st_ref[h] = s0_ref[0, h].T
            return carry
        lax.fori_loop(0, HGRN_HEADS, init, 0)

    row = lax.broadcasted_iota(jnp.int32, (c, HGRN_HD), 0)
    tt = lax.broadcasted_iota(jnp.int32, (c, c), 0)
    ss = lax.broadcasted_iota(jnp.int32, (c, c), 1)
    tri = ((ss >= tt) if rev else (ss <= tt)).astype(BF16)
    same_tile = (tt >> 3) == (ss >> 3)
    bands = [jnp.logical_and(same_tile, ss == ((tt + d) if rev else (tt - d))) for d in range(HGRN_DIAG)]
    levels = []
    size = 2 * HGRN_DIAG
    while size <= c:
        half = size // 2
        sh = size.bit_length() - 1
        row_hi = (row & (size - 1)) >= half
        t_hi = (tt & (size - 1)) >= half
        s_hi = (ss & (size - 1)) >= half
        if rev:
            is_query = jnp.logical_not(row_hi)
            pair = jnp.logical_and(jnp.logical_not(t_hi), s_hi)
        else:
            is_query = row_hi
            pair = jnp.logical_and(t_hi, jnp.logical_not(s_hi))
        pair = jnp.logical_and(pair, (tt >> sh) == (ss >> sh))
        levels.append((size, is_query, jnp.where(is_query, 1.0, -1.0), pair))
        size *= 2

    def shift1(x):
        x3 = x.reshape(c // HGRN_DIAG, HGRN_DIAG, HGRN_HD)
        return pltpu.roll(x3, (HGRN_DIAG - 1) if rev else 1, 1).reshape(c, HGRN_HD)

    def chunk_math(zq, zf, v, lb, st):
        q = _silu(zq) * (HGRN_HD ** -0.5)
        f = lb + (1.0 - lb) * jax.nn.sigmoid(zf)
        k = 1.0 - f
        g = jnp.log(f)
        g1 = g.astype(BF16)
        g2 = (g - g1.astype(F32)).astype(BF16)
        cum = _dot(tri, g1) + _dot(tri, g2)
        tot = cum[0:1, :] if rev else cum[c - 1:c, :]

        z = k
        att = jnp.where(bands[0], jnp.sum(q * z, axis=-1, keepdims=True), 0.0)
        for d in range(1, HGRN_DIAG):
            z = shift1(z) * f
            att = jnp.where(bands[d], jnp.sum(q * z, axis=-1, keepdims=True), att)

        for size, is_query, sign, pair in levels:
            half = size // 2
            pieces = []
            for b in range(c // size):
                ref_row = b * size + (half if rev else half - 1)
                pieces.append(jnp.broadcast_to(cum[ref_row:ref_row + 1, :], (size, HGRN_HD)))
            ref = pieces[0] if len(pieces) == 1 else jnp.concatenate(pieces, axis=0)
            x = (jnp.where(is_query, q, k) * jnp.exp((cum - ref) * sign)).astype(BF16)
            att = jnp.where(pair, _dot_nt(x, x), att)

        o = _dot_nt((q * jnp.exp(cum)).astype(BF16), st.astype(BF16)) + _dot(att.astype(BF16), v.astype(BF16))

        ke = (k * jnp.exp(tot - cum)).astype(BF16)
        st_new = jnp.exp(tot) * st + _dot(v.T.astype(BF16), ke)
        return o.astype(BF16), st_new

    ngrp = HGRN_HEADS // HGRN_UNROLL

    def body(step, carry):
        ci = step // ngrp
        ci = (nsub - 1 - ci) if rev else ci
        h0 = (step % ngrp) * HGRN_UNROLL
        r0 = pl.multiple_of(ci * c, c)
        rows = pl.ds(r0, c)
        ins = []
        for u in range(HGRN_UNROLL):
            h = h0 + u
            ins.append((zq_ref[0, h, rows, :].astype(F32), zf_ref[0, h, rows, :].astype(F32),
                        zv_ref[0, h, rows, :].astype(F32), lb_ref[h], st_ref[h]))
        outs = [chunk_math(*a) for a in ins]
        for u in range(HGRN_UNROLL):
            o_ref[h0 + u, rows, :] = outs[u][0]
            st_ref[h0 + u] = outs[u][1]
        return carry

    lax.fori_loop(0, nsub * ngrp, body, 0)

    @pl.when(is_prompt)
    def _():
        def fin(h, carry):
            sfin_ref[0, h] = st_ref[h].T
            return carry
        lax.fori_loop(0, HGRN_HEADS, fin, 0)


def _hgrn_scan(p_hm, lb_d, s0_d, rev):
    def blk_of(i):
        return (HGRN_NBLK - 1 - i) if rev else i

    def sec_spec(sec):
        return pl.BlockSpec((1, HGRN_HEADS, HGRN_BLOCK, HGRN_HD), lambda i: (sec, 0, blk_of(i), 0))

    def seq_idx(i):
        return _hgrn_seq_of_block(blk_of(i))

    return pl.pallas_call(
        functools.partial(_hgrn_scan_kernel, rev=rev),
        grid=(HGRN_NBLK,),
        in_specs=[
            sec_spec(0), sec_spec(2 if rev else 1), sec_spec(3),
            pl.BlockSpec((HGRN_HEADS, 1, HGRN_HD), lambda i: (0, 0, 0)),
            pl.BlockSpec((1, HGRN_HEADS, HGRN_HD, HGRN_HD),
                         lambda i: (jnp.maximum(seq_idx(i) - N_PROMPT_SEQ, 0), 0, 0, 0)),
        ],
        out_specs=[
            pl.BlockSpec((HGRN_HEADS, HGRN_BLOCK, HGRN_HD), lambda i: (0, blk_of(i), 0)),
            pl.BlockSpec((1, HGRN_HEADS, HGRN_HD, HGRN_HD),
                         lambda i: (jnp.minimum(seq_idx(i), N_PROMPT_SEQ - 1), 0, 0, 0)),
        ],
        out_shape=[
            jax.ShapeDtypeStruct((HGRN_HEADS, NT_TOK, HGRN_HD), BF16),
            jax.ShapeDtypeStruct((N_PROMPT_SEQ, HGRN_HEADS, HGRN_HD, HGRN_HD), F32),
        ],
        scratch_shapes=[pltpu.VMEM((HGRN_HEADS, HGRN_HD, HGRN_HD), F32)],
        compiler_params=_cparams(("arbitrary",)),
        name="hgrn_scan_bwd" if rev else "hgrn_scan_fwd",
    )(p_hm, p_hm, p_hm, lb_d, s0_d)


def _hgrn_out_kernel(of_ref, ob_ref, zg_ref, ng_ref, y_ref, g1_ref, gate_ref, w_ref, o_ref, x_ref):
    ng = ng_ref[...]
    for h in range(HGRN_HEADS):
        o = of_ref[h].astype(F32) + ob_ref[h].astype(F32)
        x_ref[:, h * HGRN_HD:(h + 1) * HGRN_HD] = (_rms(o, ng) * _silu(zg_ref[0, h].astype(F32))).astype(BF16)
    o_ref[...] = _post(y_ref, _dot(x_ref[...], w_ref[...]), g1_ref, gate_ref)


def _hgrn_out(o_f, o_b, p_hm, hgrn_norm_g, y, norm_l, mods_l, w):
    head_spec = pl.BlockSpec((HGRN_HEADS, TM, HGRN_HD), lambda i: (0, i, 0))
    return pl.pallas_call(
        _hgrn_out_kernel,
        grid=(NT_TOK // TM,),
        in_specs=[
            head_spec, head_spec,
            pl.BlockSpec((1, HGRN_HEADS, TM, HGRN_HD), lambda i: (4, 0, i, 0)),
            pl.BlockSpec((1, HGRN_HD), lambda i: (0, 0)),
            _tok_spec(D_MODEL), _norm_spec(1), _mod_spec(2),
            _resident((D_MODEL, D_MODEL)),
        ],
        out_specs=_tok_spec(D_MODEL),
        out_shape=jax.ShapeDtypeStruct((NT_TOK, D_MODEL), F32),
        scratch_shapes=[pltpu.VMEM((TM, D_MODEL), BF16)],
        compiler_params=_cparams(("parallel",)),
        name="hgrn_out",
    )(o_f, o_b, p_hm, hgrn_norm_g, y, norm_l, mods_l, w)


def _mla_down_kernel(y_ref, g0_ref, sh_ref, sc_ref, w_ref, qg_ref, kvg_ref, cos_ref, sin_ref,
                     cq_ref, ckv_ref, kpe_ref, kpe128_ref):
    p = _dot(_prenorm(y_ref, g0_ref, sh_ref, sc_ref), w_ref[...])
    cq_ref[...] = _rms(p[:, :MLA_Q_LORA], qg_ref[...]).astype(BF16)
    ckv_ref[...] = _rms(p[:, MLA_Q_LORA:MLA_Q_LORA + MLA_KV_LORA], kvg_ref[...])
    tail = p[:, MLA_Q_LORA + MLA_KV_LORA:]
    kpe_ref[...] = tail[:, :MLA_ROPE]
    kpe128_ref[...] = _rope128(tail, cos_ref[...], sin_ref[...]).astype(BF16)


def _mla_down(y, norm_l, mods_l, w, qg, kvg, cos_t, sin_t):
    return pl.pallas_call(
        _mla_down_kernel,
        grid=(NT_TOK // TM,),
        in_specs=[_tok_spec(D_MODEL), _norm_spec(0), _mod_spec(0), _mod_spec(1),
                  _resident((D_MODEL, MLA_DOWN_PAD)),
                  pl.BlockSpec((1, MLA_Q_LORA), lambda i: (0, 0)),
                  pl.BlockSpec((1, MLA_KV_LORA), lambda i: (0, 0)),
                  _rope_spec(), _rope_spec()],
        out_specs=[_tok_spec(MLA_Q_LORA), _tok_spec(MLA_KV_LORA), _tok_spec(MLA_ROPE), _tok_spec(128)],
        out_shape=[
            jax.ShapeDtypeStruct((NT_TOK, MLA_Q_LORA), BF16),
            jax.ShapeDtypeStruct((NT_TOK, MLA_KV_LORA), F32),
            jax.ShapeDtypeStruct((NT_TOK, MLA_ROPE), F32),
            jax.ShapeDtypeStruct((NT_TOK, 128), BF16),
        ],
        compiler_params=_cparams(("parallel",)),
        name="mla_down",
    )(y, norm_l, mods_l, mods_l, w, qg, kvg, cos_t, sin_t)


MLA_HB = 4


def _mla_q_kernel(cq_ref, w_ref, cos_ref, sin_ref, q_ref):
    acc = _dot(cq_ref[...], w_ref[0]) * MLA_Q_SCALE
    cos = cos_ref[...]
    sin = sin_ref[...]
    for hh in range(MLA_HB):
        c0 = hh * MLA_QK_PAD
        q_ref[hh, :, :MLA_NOPE] = acc[:, c0:c0 + MLA_NOPE].astype(BF16)
        q_ref[hh, :, MLA_NOPE:] = _rope128(acc[:, c0 + MLA_NOPE:c0 + MLA_QK_PAD], cos, sin).astype(BF16)


def _mla_q(cq, w_uq_g, cos_t, sin_t):
    return pl.pallas_call(
        _mla_q_kernel,
        grid=(NT_TOK // TM, MLA_HEADS // MLA_HB),
        in_specs=[_tok_spec(MLA_Q_LORA),
                  pl.BlockSpec((1, MLA_Q_LORA, MLA_HB * MLA_QK_PAD), lambda i, h: (h, 0, 0)),
                  _rope_spec(), _rope_spec()],
        out_specs=pl.BlockSpec((MLA_HB, TM, MLA_QK_PAD), lambda i, h: (h, i, 0)),
        out_shape=jax.ShapeDtypeStruct((MLA_HEADS, NT_TOK, MLA_QK_PAD), BF16),
        compiler_params=_cparams(("parallel", "arbitrary")),
        name="mla_q",
    )(cq, w_uq_g, cos_t, sin_t)


def _mla_kv_kernel(ckv_ref, kpe_ref, w_ref, k_ref, vt_ref):
    acc = _dot(ckv_ref[...], w_ref[0])
    kpe = kpe_ref[...]
    for hh in range(MLA_HB):
        c0 = hh * (MLA_NOPE + MLA_V)
        k_ref[hh, :, :MLA_NOPE] = acc[:, c0:c0 + MLA_NOPE].astype(BF16)
        k_ref[hh, :, MLA_NOPE:] = kpe
        vt_ref[hh] = acc[:, c0 + MLA_NOPE:c0 + MLA_NOPE + MLA_V].T.astype(BF16)


def _mla_kv(ckv_all, kpe_all, w_ukv_g):
    n = ckv_all.shape[0]
    return pl.pallas_call(
        _mla_kv_kernel,
        grid=(n // TM, MLA_HEADS // MLA_HB),
        in_specs=[_tok_spec(MLA_KV_LORA), _tok_spec(128),
                  pl.BlockSpec((1, MLA_KV_LORA, MLA_HB * (MLA_NOPE + MLA_V)), lambda i, h: (h, 0, 0))],
        out_specs=[pl.BlockSpec((MLA_HB, TM, MLA_QK_PAD), lambda i, h: (h, i, 0)),
                   pl.BlockSpec((MLA_HB, MLA_V, TM), lambda i, h: (h, 0, i))],
        out_shape=[jax.ShapeDtypeStruct((MLA_HEADS, n, MLA_QK_PAD), BF16),
                   jax.ShapeDtypeStruct((MLA_HEADS, MLA_V, n), BF16)],
        compiler_params=_cparams(("parallel", "arbitrary")),
        name="mla_kv",
    )(ckv_all, kpe_all, w_ukv_g)


def _row_partial(x, reduce_fn, combine_fn):
    rows, n = x.shape
    x3 = x.reshape(rows // 8, 8, n)
    parts = min(ROW_REDUCE_PARTS, rows // 8)
    bounds = [i * (rows // 8) // parts for i in range(parts + 1)]
    acc = [reduce_fn(x3[lo:hi], axis=0) for lo, hi in zip(bounds[:-1], bounds[1:])]
    while len(acc) > 1:
        acc = [combine_fn(a, b) for a, b in zip(acc[0::2], acc[1::2])] + ([acc[-1]] if len(acc) % 2 else [])
    return acc[0]


def _mla_attn_kernel(q_ref, k_ref, vt_ref, *rest, bounds):
    o_ref = rest[-1]
    q = q_ref[0]
    m = den = acc = None
    for lo, hi in zip(bounds[:-1], bounds[1:]):
        s = _dot_nt(k_ref[0, lo:hi, :], q)
        mc = jnp.max(_row_partial(s, jnp.max, jnp.maximum), axis=0, keepdims=True)
        if m is None:
            m = mc
            p = jnp.exp2(s - m)
            den = jnp.sum(_row_partial(p, jnp.sum, jnp.add), axis=0, keepdims=True)
            acc = _dot(vt_ref[0, :, lo:hi], p.astype(BF16))
        else:
            m_new = jnp.maximum(m, mc)
            alpha = jnp.exp2(m - m_new)
            p = jnp.exp2(s - m_new)
            den = alpha * den + jnp.sum(_row_partial(p, jnp.sum, jnp.add), axis=0, keepdims=True)
            acc = alpha * acc + _dot(vt_ref[0, :, lo:hi], p.astype(BF16))
            m = m_new
    o_ref[...] = (acc / den).T.astype(BF16)


MLA_TQ = 512
ROW_REDUCE_PARTS = 4


def _mla_attn_pipe_kernel(q_ref, k_ref, vt_ref, o_ref, s_ref):
    tq = MLA_TQ
    nt = SAMPLE_LEN // tq
    chunks = list(zip(MLA_KEY_CHUNKS[:-1], MLA_KEY_CHUNKS[1:]))

    def tile_rows(t):
        return pl.ds(t * tq, tq) if isinstance(t, int) else pl.ds(pl.multiple_of(t * tq, tq), tq)

    def stage(t_a, slot_a, t_b, slot_b, m_b):
        if t_a is not None:
            q = q_ref[0, tile_rows(t_a), :]
        m_parts = []
        den = acc = None
        for lo, hi in chunks:
            if t_a is not None:
                s = _dot_nt(k_ref[0, lo:hi, :], q)
                s_ref[slot_a, lo:hi, :] = s
                m_parts.append(_row_partial(s, jnp.max, jnp.maximum))
            if t_b is not None:
                p = jnp.exp2(s_ref[slot_b, lo:hi, :] - m_b)
                d = _row_partial(p, jnp.sum, jnp.add)
                a = _dot(vt_ref[0, :, lo:hi], p.astype(BF16))
                den = d if den is None else den + d
                acc = a if acc is None else acc + a
        if t_b is not None:
            den = jnp.sum(den, axis=0, keepdims=True)
            o_ref[tile_rows(t_b), :] = (acc / den).T.astype(BF16)
        if t_a is None:
            return None
        m_a = m_parts[0]
        for part in m_parts[1:]:
            m_a = jnp.maximum(m_a, part)
        return jnp.max(m_a, axis=0, keepdims=True)

    m = stage(0, 0, None, None, None)

    def body(j, m_prev):
        t = 2 * j + 1
        m_mid = stage(t, 1, t - 1, 0, m_prev)
        return stage(t + 1, 0, t, 1, m_mid)

    m = lax.fori_loop(0, (nt - 2) // 2, body, m)
    m = stage(nt - 1, 1, nt - 2, 0, m)
    stage(None, None, nt - 1, 1, m)


def _mla_attn(q_hm, k_hm, vt_hm):
    pk0 = N_SAMPLE_SEQ * MLA_KV_LEN // PROMPT_LEN
    o_p = pl.pallas_call(
        functools.partial(_mla_attn_kernel, bounds=(0, PROMPT_LEN)),
        grid=(MLA_HEADS, N_PROMPT_SEQ),
        in_specs=[pl.BlockSpec((1, PROMPT_LEN, MLA_QK_PAD), lambda h, b: (h, b, 0)),
                  pl.BlockSpec((1, PROMPT_LEN, MLA_QK_PAD), lambda h, b: (h, pk0 + b, 0)),
                  pl.BlockSpec((1, MLA_V, PROMPT_LEN), lambda h, b: (h, 0, pk0 + b))],
        out_specs=pl.BlockSpec((PROMPT_LEN, MLA_V), lambda h, b: (b, h)),
        out_shape=jax.ShapeDtypeStruct((NP_TOK, MLA_HEADS * MLA_V), BF16),
        compiler_params=_cparams(("parallel", "parallel")),
        name="mla_attn_prompt",
    )(q_hm, k_hm, vt_hm)
    q0 = NP_TOK // SAMPLE_LEN
    o_s = pl.pallas_call(
        _mla_attn_pipe_kernel,
        grid=(MLA_HEADS, N_SAMPLE_SEQ),
        in_specs=[pl.BlockSpec((1, SAMPLE_LEN, MLA_QK_PAD), lambda h, b: (h, q0 + b, 0)),
                  pl.BlockSpec((1, MLA_KV_LEN, MLA_QK_PAD), lambda h, b: (h, b, 0)),
                  pl.BlockSpec((1, MLA_V, MLA_KV_LEN), lambda h, b: (h, 0, b))],
        out_specs=pl.BlockSpec((SAMPLE_LEN, MLA_V), lambda h, b: (b, h)),
        out_shape=jax.ShapeDtypeStruct((NS_TOK, MLA_HEADS * MLA_V), BF16),
        scratch_shapes=[pltpu.VMEM((2, MLA_KV_LEN, MLA_TQ), F32)],
        compiler_params=_cparams(("parallel", "parallel")),
        name="mla_attn_sample",
    )(q_hm, k_hm, vt_hm)
    return o_p, o_s


def _swa_qkv_kernel(y_ref, g0_ref, sh_ref, sc_ref, w_ref, cos_ref, sin_ref, q_ref, kv_ref, kf_ref, vf_ref):
    p = _dot(_prenorm(y_ref, g0_ref, sh_ref, sc_ref), w_ref[...])
    cos = cos_ref[...]
    sin = sin_ref[...]
    kf_ref[...] = p[:, SWA_QW:SWA_QW + SWA_KVW]
    vf_ref[...] = p[:, SWA_QW + SWA_KVW:]
    for c in range(SWA_QW // 128):
        r = _rope128(p[:, c * 128:(c + 1) * 128], cos, sin).astype(BF16)
        q_ref[2 * c] = r[:, :SWA_HD]
        q_ref[2 * c + 1] = r[:, SWA_HD:]
    for c in range(SWA_KVW // 128):
        k0 = SWA_QW + c * 128
        v0 = SWA_QW + SWA_KVW + c * 128
        k = _rope128(p[:, k0:k0 + 128], cos, sin).astype(BF16)
        v = p[:, v0:v0 + 128].astype(BF16)
        kv_ref[2 * c] = jnp.concatenate([k[:, :SWA_HD], v[:, :SWA_HD]], axis=1)
        kv_ref[2 * c + 1] = jnp.concatenate([k[:, SWA_HD:], v[:, SWA_HD:]], axis=1)


def _swa_qkv(y, norm_l, mods_l, w, cos_t, sin_t):
    return pl.pallas_call(
        _swa_qkv_kernel,
        grid=(NT_TOK // TM,),
        in_specs=[_tok_spec(D_MODEL), _norm_spec(0), _mod_spec(0), _mod_spec(1),
                  _resident((D_MODEL, SWA_QW + 2 * SWA_KVW)), _rope_spec(), _rope_spec()],
        out_specs=[pl.BlockSpec((SWA_Q_HEADS, TM, SWA_HD), lambda i: (0, i, 0)),
                   pl.BlockSpec((SWA_KV_HEADS, TM, 2 * SWA_HD), lambda i: (0, i, 0)),
                   _tok_spec(SWA_KVW), _tok_spec(SWA_KVW)],
        out_shape=[
            jax.ShapeDtypeStruct((SWA_Q_HEADS, NT_TOK, SWA_HD), BF16),
            jax.ShapeDtypeStruct((SWA_KV_HEADS, NT_TOK, 2 * SWA_HD), BF16),
            jax.ShapeDtypeStruct((NT_TOK, SWA_KVW), F32),
            jax.ShapeDtypeStruct((NT_TOK, SWA_KVW), F32),
        ],
        compiler_params=_cparams(("parallel",)),
        name="swa_qkv",
    )(y, norm_l, mods_l, mods_l, w, cos_t, sin_t)


def _swa_attn_kernel(sink_ref, q_ref, kvc_ref, *rest, local):
    o_ref = rest[-1]
    kv0 = pl.program_id(0) * SWA_KVB
    tq = q_ref.shape[1]
    nq = SWA_GROUP * tq
    kvc = kvc_ref[...].reshape(SWA_KVB, kvc_ref.shape[-2], 2 * SWA_HD)
    head = lax.broadcasted_iota(jnp.int32, (1, nq), 1) >> (tq.bit_length() - 1)
    if local:
        j = pl.program_id(2)
        nb = pl.num_programs(2)
        key = lax.broadcasted_iota(jnp.int32, (SWA_BLOCK, nq), 0)
        qry = lax.broadcasted_iota(jnp.int32, (SWA_BLOCK, nq), 1) & (SWA_BLOCK - 1)
        keep = [jnp.logical_and(key >= qry, j > 0), None, jnp.logical_and(key <= qry, j < nb - 1)]
    for u in range(SWA_KVB):
        q = q_ref[u * SWA_GROUP:(u + 1) * SWA_GROUP].reshape(nq, SWA_HD) * (SWA_HD ** -0.5)
        sink = jnp.zeros((1, nq), F32)
        for g in range(SWA_GROUP):
            sink = jnp.where(head == g, sink_ref[(kv0 + u) * SWA_GROUP + g], sink)
        blocks = [kvc[u]]
        scores = [_dot_nt(blocks[0][:, :SWA_HD], q)]
        if local:
            for t in range(3):
                blk = rest[t][u]
                s = _dot_nt(blk[:, :SWA_HD], q)
                scores.append(s if keep[t] is None else jnp.where(keep[t], s, NEG_INF))
                blocks.append(blk)
        s = scores[0] if len(scores) == 1 else jnp.concatenate(scores, axis=0)
        kv_all = blocks[0] if len(blocks) == 1 else jnp.concatenate(blocks, axis=0)
        m = jnp.maximum(jnp.max(_row_partial(s, jnp.max, jnp.maximum), axis=0, keepdims=True), sink)
        p = jnp.exp(s - m)
        den = jnp.sum(_row_partial(p, jnp.sum, jnp.add), axis=0, keepdims=True) + jnp.exp(sink - m)
        kv_t = kv_all.astype(F32).T.astype(BF16)
        out = (_dot(kv_t, p.astype(BF16)) / den).T.astype(BF16)
        for g in range(SWA_GROUP):
            c0 = (u * SWA_GROUP + g) * SWA_HD
            o_ref[:, c0:c0 + SWA_HD] = out[g * tq:(g + 1) * tq, SWA_HD:]


def _swa_attn(sink, q_hm, kv_hm, kvc_s):
    smem = pl.BlockSpec(memory_space=pltpu.SMEM)
    gw = SWA_KVB * SWA_GROUP * SWA_HD
    qb = SWA_KVB * SWA_GROUP
    o_p = pl.pallas_call(
        functools.partial(_swa_attn_kernel, local=False),
        grid=(SWA_KV_HEADS // SWA_KVB, N_PROMPT_SEQ),
        in_specs=[smem,
                  pl.BlockSpec((qb, PROMPT_LEN, SWA_HD), lambda kv, b: (kv, b, 0)),
                  pl.BlockSpec((SWA_KVB, PROMPT_LEN, 2 * SWA_HD), lambda kv, b: (kv, b, 0))],
        out_specs=pl.BlockSpec((PROMPT_LEN, gw), lambda kv, b: (b, kv)),
        out_shape=jax.ShapeDtypeStruct((NP_TOK, SWA_QW), BF16),
        compiler_params=_cparams(("parallel", "parallel")),
        name="swa_attn_prompt",
    )(sink, q_hm, kv_hm)
    nb = SAMPLE_LEN // SWA_BLOCK
    b0 = NP_TOK // SWA_BLOCK

    def loc(off):
        return pl.BlockSpec((SWA_KVB, SWA_BLOCK, 2 * SWA_HD),
                            lambda kv, b, j: (kv, b0 + b * nb + jnp.clip(j + off, 0, nb - 1), 0))

    o_s = pl.pallas_call(
        functools.partial(_swa_attn_kernel, local=True),
        grid=(SWA_KV_HEADS // SWA_KVB, N_SAMPLE_SEQ, nb),
        in_specs=[smem,
                  pl.BlockSpec((qb, SWA_BLOCK, SWA_HD), lambda kv, b, j: (kv, b0 + b * nb + j, 0)),
                  pl.BlockSpec((1, SWA_KVB, PAST_LEN, 2 * SWA_HD), lambda kv, b, j: (b, kv, 0, 0)),
                  loc(-1), loc(0), loc(1)],
        out_specs=pl.BlockSpec((SWA_BLOCK, gw), lambda kv, b, j: (b * nb + j, kv)),
        out_shape=jax.ShapeDtypeStruct((NS_TOK, SWA_QW), BF16),
        compiler_params=_cparams(("parallel", "parallel", "arbitrary")),
        name="swa_attn_sample",
    )(sink, q_hm, kvc_s, kv_hm, kv_hm, kv_hm)
    return o_p, o_s


def _rope_tables():
    t = jnp.arange(SAMPLE_LEN, dtype=jnp.int32)
    inv = ROPE_BASE ** (-jnp.arange(16, dtype=F32) / 16)
    ang_r = (t // GRID_W).astype(F32)[:, None] * inv[None, :]
    ang_c = (t % GRID_W).astype(F32)[:, None] * inv[None, :]
    cos = jnp.concatenate([jnp.cos(ang_r), jnp.cos(ang_r), jnp.cos(ang_c), jnp.cos(ang_c)], axis=-1)
    sin = jnp.concatenate([-jnp.sin(ang_r), jnp.sin(ang_r), -jnp.sin(ang_c), jnp.sin(ang_c)], axis=-1)
    cos = jnp.concatenate([cos, cos], axis=-1)
    sin = jnp.concatenate([sin, sin], axis=-1)
    cos_t = jnp.concatenate([jnp.ones((NP_TOK, 128), F32), cos], axis=0)
    sin_t = jnp.concatenate([jnp.zeros((NP_TOK, 128), F32), sin], axis=0)
    return cos_t, sin_t


def kernel(x_prompt, x_sample, c, state_hgrn, cache_mla_ckv, cache_mla_kpe, cache_swa_k, cache_swa_v, c_ctx, ada_w, ada_b, norm_g, mlp_w_in, mlp_w_out, hgrn_w_in, hgrn_lb_logits, hgrn_norm_g, hgrn_w_out, mla_w_down, mla_q_norm_g, mla_kv_norm_g, mla_w_uq, mla_w_ukv, mla_w_out, swa_w_qkv, swa_sink, swa_w_out):
    y = jnp.concatenate([x_prompt.reshape(NP_TOK, D_MODEL), x_sample.reshape(NS_TOK, D_MODEL)], axis=0)
    cond8 = jnp.concatenate([c_ctx[None, :], c, jnp.zeros((3, D_MODEL), F32)], axis=0)
    mods = _adaln(cond8, ada_w, ada_b).reshape(DEPTH, 8, N_MOD, 1, D_MODEL)
    cos_t, sin_t = _rope_tables()

    lb_soft = jax.nn.softmax(hgrn_lb_logits.astype(F32), axis=1)
    lb_all = jnp.cumsum(lb_soft, axis=1) - lb_soft[:, :1]

    new_hgrn = []
    new_ckv = new_kpe = new_k = new_v = None
    for layer in range(DEPTH):
        kind, j = layer % 3, layer // 3
        norm_l = norm_g[layer].reshape(4, 1, D_MODEL)
        mods_l = mods[layer]
        if kind == 0:
            p_hm = _hgrn_in(y, norm_l, mods_l, hgrn_w_in[j].astype(BF16))
            lb = lb_all[:, layer].reshape(2, HGRN_HEADS, 1, HGRN_HD)
            o_f, s_f = _hgrn_scan(p_hm, lb[0], state_hgrn[:, j, 0], rev=False)
            o_b, s_b = _hgrn_scan(p_hm, lb[1], state_hgrn[:, j, 1], rev=True)
            new_hgrn.append(jnp.stack([s_f, s_b], axis=1))
            y = _hgrn_out(o_f, o_b, p_hm, hgrn_norm_g[j].reshape(1, HGRN_HD), y, norm_l, mods_l,
                          hgrn_w_out[j].astype(BF16))
        elif kind == 1:
            w_down = jnp.pad(mla_w_down[j], ((0, 0), (0, MLA_DOWN_PAD - mla_w_down.shape[-1]))).astype(BF16)
            cq, ckv, kpe, kpe128 = _mla_down(y, norm_l, mods_l, w_down, mla_q_norm_g[j].reshape(1, -1),
                                             mla_kv_norm_g[j].reshape(1, -1), cos_t, sin_t)
            ng = MLA_HEADS // MLA_HB
            w_uq = jnp.pad(mla_w_uq[j].reshape(MLA_Q_LORA, MLA_HEADS, MLA_QK),
                           ((0, 0), (0, 0), (0, MLA_QK_PAD - MLA_QK)))
            w_uq = w_uq.reshape(MLA_Q_LORA, ng, MLA_HB * MLA_QK_PAD).transpose(1, 0, 2).astype(BF16)
            w_ukv = mla_w_ukv[j].reshape(MLA_KV_LORA, ng, MLA_HB * (MLA_NOPE + MLA_V)).transpose(1, 0, 2).astype(BF16)
            q_hm = _mla_q(cq, w_uq, cos_t, sin_t)
            ckv_s = jnp.concatenate([cache_mla_ckv[:, j].astype(BF16),
                                     ckv[NP_TOK:].reshape(N_SAMPLE_SEQ, SAMPLE_LEN, MLA_KV_LORA).astype(BF16)], axis=1)
            kpe_ctx = jnp.pad(cache_mla_kpe[:, j], ((0, 0), (0, 0), (0, 128 - MLA_ROPE))).astype(BF16)
            kpe_s = jnp.concatenate([kpe_ctx, kpe128[NP_TOK:].reshape(N_SAMPLE_SEQ, SAMPLE_LEN, 128)], axis=1)
            ckv_all = jnp.concatenate([ckv_s.reshape(-1, MLA_KV_LORA), ckv[:NP_TOK].astype(BF16)], axis=0)
            kpe_all = jnp.concatenate([kpe_s.reshape(-1, 128), kpe128[:NP_TOK]], axis=0)
            k_hm, v_hm = _mla_kv(ckv_all, kpe_all, w_ukv)
            o_p, o_s = _mla_attn(q_hm, k_hm, v_hm)
            new_ckv = ckv[:NP_TOK].reshape(N_PROMPT_SEQ, 1, PROMPT_LEN, MLA_KV_LORA)
            new_kpe = kpe[:NP_TOK].reshape(N_PROMPT_SEQ, 1, PROMPT_LEN, MLA_ROPE)
            y = _outproj(o_p, o_s, y, norm_l, mods_l, mla_w_out[j].astype(BF16))
        else:
            q_hm, kv_hm, kf, vf = _swa_qkv(y, norm_l, mods_l, swa_w_qkv[j].astype(BF16), cos_t, sin_t)
            kvc_s = jnp.concatenate([cache_swa_k[:, j], cache_swa_v[:, j]], axis=-1).transpose(0, 2, 1, 3).astype(BF16)
            o_p, o_s = _swa_attn(swa_sink[j], q_hm, kv_hm, kvc_s)
            new_k = kf[:NP_TOK].reshape(N_PROMPT_SEQ, 1, PROMPT_LEN, SWA_KV_HEADS, SWA_HD)
            new_v = vf[:NP_TOK].reshape(N_PROMPT_SEQ, 1, PROMPT_LEN, SWA_KV_HEADS, SWA_HD)
            y = _outproj(o_p, o_s, y, norm_l, mods_l, swa_w_out[j].astype(BF16))
        w1 = mlp_w_in[layer].astype(BF16)
        w2 = mlp_w_out[layer].astype(BF16)
        if layer < DEPTH - 1:
            y = _mlp(y, norm_l, mods_l, w1, w2, 0, NT_TOK // TM)
        else:
            y_prompt = _mlp(y, norm_l, mods_l, w1, w2, 0, PROMPT_TILES)
            y_sample = _mlp(y, norm_l, mods_l, w1, w2, PROMPT_TILES, NS_TOK // TM)

    return (y_prompt.reshape(N_PROMPT_SEQ, PROMPT_LEN, D_MODEL), y_sample.reshape(N_SAMPLE_SEQ, SAMPLE_LEN, D_MODEL),
            jnp.stack(new_hgrn, axis=1), new_ckv, new_kpe, new_k, new_v)
```

```python
import functools

import jax
import jax.numpy as jnp
from jax import lax
from jax.experimental import pallas as pl
from jax.experimental.pallas import tpu as pltpu

F32 = jnp.float32
BF16 = jnp.bfloat16

D_MODEL = 2048
DEPTH = 4
N_PROMPT_SEQ = 16
PROMPT_LEN = 256
N_SAMPLE_SEQ = 4
SAMPLE_LEN = 4096
PAST_LEN = 256
NP_TOK = N_PROMPT_SEQ * PROMPT_LEN
NS_TOK = N_SAMPLE_SEQ * SAMPLE_LEN
NT_TOK = NP_TOK + NS_TOK
GROUP_TOK = 4096
N_GROUPS = NT_TOK // GROUP_TOK
N_MOD = 6
D_FF = 4 * D_MODEL
NORM_EPS = 1e-6
GRID_W = 64
ROPE_BASE = 10000.0

HGRN_HEADS = 16
HGRN_HD = 128
HGRN_CHUNK = 128
HGRN_BLOCK = 256
SUBLANES = 8
HGRN_DIAG = 8
HGRN_UNROLL = 8

MLA_HEADS = 16
MLA_Q_LORA = 512
MLA_KV_LORA = 512
MLA_NOPE = 128
MLA_ROPE = 64
MLA_V = 128
MLA_QK = MLA_NOPE + MLA_ROPE
MLA_QK_PAD = 256
MLA_DOWN_PAD = 1152
MLA_KV_LEN = PAST_LEN + SAMPLE_LEN
MLA_KEY_CHUNKS = (0, 640, 1280, 1792, 2304, 2816, 3328, 3840, MLA_KV_LEN)
MLA_Q_SCALE = MLA_QK ** -0.5 * 1.4426950408889634

SWA_HD = 64
SWA_Q_HEADS = 32
SWA_KV_HEADS = 8
SWA_GROUP = SWA_Q_HEADS // SWA_KV_HEADS
SWA_QW = SWA_Q_HEADS * SWA_HD
SWA_KVW = SWA_KV_HEADS * SWA_HD
SWA_BLOCK = 128
SWA_KVB = 8
NEG_INF = -1e30

TM = 512
TILES_PER_GROUP = GROUP_TOK // TM
VMEM_LIMIT = 56 * 1024 * 1024


def _cparams(sem):
    return pltpu.CompilerParams(dimension_semantics=sem, vmem_limit_bytes=VMEM_LIMIT)


def _dot(a, b):
    return jnp.dot(a, b, preferred_element_type=F32)


def _dot_nt(a, b):
    return lax.dot_general(a, b, (((1,), (1,)), ((), ())), preferred_element_type=F32)


def _rms(x, g):
    return x * lax.rsqrt(jnp.mean(x * x, axis=-1, keepdims=True) + NORM_EPS) * g


def _silu(x):
    return x * jax.nn.sigmoid(x)


def _prenorm(y_ref, g_ref, sh_ref, sc_ref):
    return (_rms(y_ref[...], g_ref[0]) * (1.0 + sc_ref[0, 0]) + sh_ref[0, 0]).astype(BF16)


def _post(y_ref, acc, g_ref, gate_ref):
    return y_ref[...] + gate_ref[0, 0] * _rms(acc, g_ref[0])


def _rope128(x, cos, sin):
    lane = lax.broadcasted_iota(jnp.int32, x.shape, 1)
    first = (lane & 31) < 16
    swapped = jnp.where(first, pltpu.roll(x, 112, 1), pltpu.roll(x, 16, 1))
    return x * cos + swapped * sin


def _tok_spec(n):
    return pl.BlockSpec((TM, n), lambda i, *_: (i, 0))


def _norm_spec(k):
    return pl.BlockSpec((1, 1, D_MODEL), lambda i, *_: (k, 0, 0))


def _mod_spec(k):
    return pl.BlockSpec((1, 1, 1, D_MODEL), lambda i, *_: (i // TILES_PER_GROUP, k, 0, 0))


def _rope_spec():
    def idx(i, *_):
        return (jnp.where(i < TILES_PER_GROUP, i, TILES_PER_GROUP + i % TILES_PER_GROUP), 0)
    return pl.BlockSpec((TM, 128), idx)


def _resident(shape):
    nd = len(shape)
    return pl.BlockSpec(shape, lambda *_: (0,) * nd, pipeline_mode=pl.Buffered(1))


def _adaln_kernel(c_ref, w_ref, b_ref, o_ref):
    s = _silu(c_ref[...])
    s_hi = s.astype(BF16)
    s_lo = (s - s_hi.astype(F32)).astype(BF16)
    w = w_ref[0]
    w_hi = w.astype(BF16)
    w_lo = (w - w_hi.astype(F32)).astype(BF16)
    o_ref[0] = _dot(s_hi, w_hi) + _dot(s_lo, w_hi) + _dot(s_hi, w_lo) + b_ref[0]


def _adaln(cond8, ada_w, ada_b):
    tn = 1024
    n = N_MOD * D_MODEL
    return pl.pallas_call(
        _adaln_kernel,
        grid=(DEPTH, n // tn),
        in_specs=[
            pl.BlockSpec((8, D_MODEL), lambda l, j: (0, 0)),
            pl.BlockSpec((1, D_MODEL, tn), lambda l, j: (l, 0, j)),
            pl.BlockSpec((1, 1, tn), lambda l, j: (l, 0, j)),
        ],
        out_specs=pl.BlockSpec((1, 8, tn), lambda l, j: (l, 0, j)),
        out_shape=jax.ShapeDtypeStruct((DEPTH, 8, n), F32),
        compiler_params=_cparams(("parallel", "parallel")),
        name="adaln",
    )(cond8, ada_w, ada_b.reshape(DEPTH, 1, n))


def _prenorm_rows(y_ref, rows, g_ref, sh_ref, sc_ref):
    return (_rms(y_ref[rows, :], g_ref[0]) * (1.0 + sc_ref[0, 0]) + sh_ref[0, 0]).astype(BF16)


def _mlp_kernel(y_ref, yn_ref, g2_ref, g3_ref, sh_ref, sc_ref, shn_ref, scn_ref, gate_ref, w1_ref, w2_ref,
                o_ref, h_ref, *, nf):
    i = pl.program_id(0)
    f = pl.program_id(1)
    cur = i % 2
    all_rows = pl.ds(0, TM)

    @pl.when(jnp.logical_and(i == 0, f == 0))
    def _():
        h_ref[0] = _prenorm_rows(y_ref, all_rows, g2_ref, sh_ref, sc_ref)

    def partial_sum():
        u = jnp.maximum(_dot(h_ref[cur], w1_ref[0]), 0.0)
        part = _dot((u * u).astype(BF16), w2_ref[0])
        rows = pl.ds(pl.multiple_of(f * (TM // nf), TM // nf), TM // nf)
        h_ref[1 - cur, rows, :] = _prenorm_rows(yn_ref, rows, g2_ref, shn_ref, scn_ref)
        return part

    @pl.when(f == 0)
    def _():
        o_ref[...] = partial_sum()

    @pl.when(jnp.logical_and(f > 0, f < nf - 1))
    def _():
        o_ref[...] += partial_sum()

    @pl.when(f == nf - 1)
    def _():
        o_ref[...] = _post(y_ref, o_ref[...] + partial_sum(), g3_ref, gate_ref)


def _mlp(y, norm_l, mods_l, w1, w2, layer, tile0, ntiles):
    tf = 1024

    def nxt(i):
        return jnp.minimum(i + 1, ntiles - 1) + tile0

    def mod(k, tile_of):
        return pl.BlockSpec((1, 1, 1, D_MODEL), lambda i, f: (tile_of(i) // TILES_PER_GROUP, k, 0, 0))

    def cur(i):
        return i + tile0

    return pl.pallas_call(
        functools.partial(_mlp_kernel, nf=D_FF // tf),
        grid=(ntiles, D_FF // tf),
        in_specs=[
            pl.BlockSpec((TM, D_MODEL), lambda i, f: (cur(i), 0)),
            pl.BlockSpec((TM, D_MODEL), lambda i, f: (nxt(i), 0)),
            _norm_spec(2), _norm_spec(3),
            mod(3, cur), mod(4, cur), mod(3, nxt), mod(4, nxt), mod(5, cur),
            pl.BlockSpec((1, D_MODEL, tf), lambda i, f: (layer, 0, f)),
            pl.BlockSpec((1, tf, D_MODEL), lambda i, f: (layer, f, 0)),
        ],
        out_specs=_tok_spec(D_MODEL),
        out_shape=jax.ShapeDtypeStruct((ntiles * TM, D_MODEL), F32),
        scratch_shapes=[pltpu.VMEM((2, TM, D_MODEL), BF16)],
        compiler_params=_cparams(("arbitrary", "arbitrary")),
        name="mlp",
    )(y, y, norm_l, norm_l, mods_l, mods_l, mods_l, mods_l, mods_l, w1, w2)


PROMPT_TILES = NP_TOK // TM


def _outproj_kernel(xp_ref, xs_ref, y_ref, g1_ref, gate_ref, w_ref, o_ref):
    x = jnp.where(pl.program_id(0) < PROMPT_TILES, xp_ref[...], xs_ref[...])
    o_ref[...] = _post(y_ref, _dot(x, w_ref[...]), g1_ref, gate_ref)


def _outproj(x_p, x_s, y, norm_l, mods_l, w):
    return pl.pallas_call(
        _outproj_kernel,
        grid=(NT_TOK // TM,),
        in_specs=[pl.BlockSpec((TM, D_MODEL), lambda i: (jnp.minimum(i, PROMPT_TILES - 1), 0)),
                  pl.BlockSpec((TM, D_MODEL), lambda i: (jnp.maximum(i - PROMPT_TILES, 0), 0)),
                  _tok_spec(D_MODEL), _norm_spec(1), _mod_spec(2),
                  _resident((D_MODEL, D_MODEL))],
        out_specs=_tok_spec(D_MODEL),
        out_shape=jax.ShapeDtypeStruct((NT_TOK, D_MODEL), F32),
        compiler_params=_cparams(("parallel",)),
        name="outproj",
    )(x_p, x_s, y, norm_l, mods_l, w)


def _stream_specs(split):
    if not split:
        return [_tok_spec(D_MODEL)]
    return [pl.BlockSpec((TM, D_MODEL), lambda i, *_: (jnp.minimum(i, PROMPT_TILES - 1), 0)),
            pl.BlockSpec((TM, D_MODEL), lambda i, *_: (jnp.maximum(i - PROMPT_TILES, 0), 0))]


def _stream_tile(refs):
    if len(refs) == 1:
        return refs[0][...]
    return jnp.where(pl.program_id(0) < PROMPT_TILES, refs[0][...], refs[1][...])


def _hgrn_in_kernel(*refs):
    *y_refs, g0_ref, sh_ref, sc_ref, w_ref, o_ref, h_ref = refs

    @pl.when(pl.program_id(1) == 0)
    def _():
        y = _stream_tile(y_refs)
        h_ref[...] = (_rms(y, g0_ref[0]) * (1.0 + sc_ref[0, 0]) + sh_ref[0, 0]).astype(BF16)

    acc = _dot(h_ref[...], w_ref[0])
    for c in range(HGRN_HEADS):
        o_ref[0, c] = acc[:, c * HGRN_HD:(c + 1) * HGRN_HD].astype(BF16)


def _hgrn_in(ys, norm_l, mods_l, w, j):
    return pl.pallas_call(
        _hgrn_in_kernel,
        grid=(NT_TOK // TM, 5),
        in_specs=_stream_specs(len(ys) == 2) + [
            _norm_spec(0), _mod_spec(0), _mod_spec(1),
            pl.BlockSpec((1, D_MODEL, D_MODEL), lambda i, sec: (j, 0, sec))],
        out_specs=pl.BlockSpec((1, HGRN_HEADS, TM, HGRN_HD), lambda i, sec: (sec, 0, i, 0)),
        out_shape=jax.ShapeDtypeStruct((5, HGRN_HEADS, NT_TOK, HGRN_HD), BF16),
        scratch_shapes=[pltpu.VMEM((TM, D_MODEL), BF16)],
        compiler_params=_cparams(("parallel", "arbitrary")),
        name="hgrn_in",
    )(*ys, norm_l, mods_l, mods_l, w)


HGRN_NBLK = NT_TOK // HGRN_BLOCK
HGRN_PBLK = NP_TOK // HGRN_BLOCK
HGRN_SBLK = SAMPLE_LEN // HGRN_BLOCK
HGRN_NSEQ = N_PROMPT_SEQ + N_SAMPLE_SEQ


def _hgrn_seq_of_block(blk):
    return jnp.where(blk < HGRN_PBLK, blk, HGRN_PBLK + (blk - HGRN_PBLK) // HGRN_SBLK)


def _hgrn_scan_kernel(zq_ref, zf_ref, zv_ref, lb_ref, s0_ref, o_ref, sfin_ref, st_ref, *, rev):
    c = HGRN_CHUNK
    nsub = HGRN_BLOCK // c
    i = pl.program_id(0)
    blk = (HGRN_NBLK - 1 - i) if rev else i
    is_prompt = blk < HGRN_PBLK
    pos = (blk - HGRN_PBLK) % HGRN_SBLK
    seq_start = jnp.logical_or(is_prompt, pos == (HGRN_SBLK - 1 if rev else 0))

    @pl.when(jnp.logical_and(seq_start, is_prompt))
    def _():
        st_ref[...] = jnp.zeros_like(st_ref)

    @pl.when(jnp.logical_and(seq_start, jnp.logical_not(is_prompt)))
    def _():
        def init(h, carry):
            st_ref[h] = s0_ref[0, h].T
            return carry
        lax.fori_loop(0, HGRN_HEADS, init, 0)

    row = lax.broadcasted_iota(jnp.int32, (c, HGRN_HD), 0)
    tt = lax.broadcasted_iota(jnp.int32, (c, c), 0)
    ss = lax.broadcasted_iota(jnp.int32, (c, c), 1)
    tri = ((ss >= tt) if rev else (ss <= tt)).astype(BF16)
    dsh = HGRN_DIAG.bit_length() - 1
    same_sub = (tt >> dsh) == (ss >> dsh)
    bands = [jnp.logical_and(same_sub, ss == ((tt + d) if rev else (tt - d))) for d in range(HGRN_DIAG)]
    levels = []
    size = 2 * HGRN_DIAG
    while size <= c:
        half = size // 2
        sh = size.bit_length() - 1
        row_hi = (row & (size - 1)) >= half
        t_hi = (tt & (size - 1)) >= half
        s_hi = (ss & (size - 1)) >= half
        if rev:
            is_query = jnp.logical_not(row_hi)
            pair = jnp.logical_and(jnp.logical_not(t_hi), s_hi)
        else:
            is_query = row_hi
            pair = jnp.logical_and(t_hi, jnp.logical_not(s_hi))
        pair = jnp.logical_and(pair, (tt >> sh) == (ss >> sh))
        levels.append((size, is_query, jnp.where(is_query, 1.0, -1.0), pair))
        size *= 2

    def shift1(x):
        x3 = x.reshape(c // SUBLANES, SUBLANES, HGRN_HD)
        return pltpu.roll(x3, (SUBLANES - 1) if rev else 1, 1).reshape(c, HGRN_HD)

    def chunk_math(zq, zf, v, lb, st):
        q = _silu(zq) * (HGRN_HD ** -0.5)
        f = lb + (1.0 - lb) * jax.nn.sigmoid(zf)
        k = 1.0 - f
        g = jnp.log(f)
        g1 = g.astype(BF16)
        g2 = (g - g1.astype(F32)).astype(BF16)
        cum = _dot(tri, g1) + _dot(tri, g2)
        tot = cum[0:1, :] if rev else cum[c - 1:c, :]

        z = k
        att = jnp.where(bands[0], jnp.sum(q * z, axis=-1, keepdims=True), 0.0)
        for d in range(1, HGRN_DIAG):
            z = shift1(z) * f
            att = jnp.where(bands[d], jnp.sum(q * z, axis=-1, keepdims=True), att)

        for size, is_query, sign, pair in levels:
            half = size // 2
            pieces = []
            for b in range(c // size):
                ref_row = b * size + (half if rev else half - 1)
                pieces.append(jnp.broadcast_to(cum[ref_row:ref_row + 1, :], (size, HGRN_HD)))
            ref = pieces[0] if len(pieces) == 1 else jnp.concatenate(pieces, axis=0)
            x = (jnp.where(is_query, q, k) * jnp.exp((cum - ref) * sign)).astype(BF16)
            att = jnp.where(pair, _dot_nt(x, x), att)

        o = _dot_nt((q * jnp.exp(cum)).astype(BF16), st.astype(BF16)) + _dot(att.astype(BF16), v.astype(BF16))

        ke = (k * jnp.exp(tot - cum)).astype(BF16)
        st_new = jnp.exp(tot) * st + _dot(v.T.astype(BF16), ke)
        return o.astype(BF16), st_new

    ngrp = HGRN_HEADS // HGRN_UNROLL

    def body(step, carry):
        ci = step // ngrp
        ci = (nsub - 1 - ci) if rev else ci
        h0 = (step % ngrp) * HGRN_UNROLL
        r0 = pl.multiple_of(ci * c, c)
        rows = pl.ds(r0, c)
        ins = []
        for u in range(HGRN_UNROLL):
            h = h0 + u
            ins.append((zq_ref[0, h, rows, :].astype(F32), zf_ref[0, h, rows, :].astype(F32),
                        zv_ref[0, h, rows, :].astype(F32), lb_ref[h], st_ref[h]))
        outs = [chunk_math(*a) for a in ins]
        for u in range(HGRN_UNROLL):
            o_ref[h0 + u, rows, :] = outs[u][0]
            st_ref[h0 + u] = outs[u][1]
        return carry

    lax.fori_loop(0, nsub * ngrp, body, 0)

    @pl.when(is_prompt)
    def _():
        def fin(h, carry):
            sfin_ref[0, h] = st_ref[h].T
            return carry
        lax.fori_loop(0, HGRN_HEADS, fin, 0)


def _hgrn_scan(p_hm, lb_d, s0_d, rev):
    def blk_of(i):
        return (HGRN_NBLK - 1 - i) if rev else i

    def sec_spec(sec):
        return pl.BlockSpec((1, HGRN_HEADS, HGRN_BLOCK, HGRN_HD), lambda i: (sec, 0, blk_of(i), 0))

    def seq_idx(i):
        return _hgrn_seq_of_block(blk_of(i))

    return pl.pallas_call(
        functools.partial(_hgrn_scan_kernel, rev=rev),
        grid=(HGRN_NBLK,),
        in_specs=[
            sec_spec(0), sec_spec(2 if rev else 1), sec_spec(3),
            pl.BlockSpec((HGRN_HEADS, 1, HGRN_HD), lambda i: (0, 0, 0)),
            pl.BlockSpec((1, HGRN_HEADS, HGRN_HD, HGRN_HD),
                         lambda i: (jnp.maximum(seq_idx(i) - N_PROMPT_SEQ, 0), 0, 0, 0)),
        ],
        out_specs=[
            pl.BlockSpec((HGRN_HEADS, HGRN_BLOCK, HGRN_HD), lambda i: (0, blk_of(i), 0)),
            pl.BlockSpec((1, HGRN_HEADS, HGRN_HD, HGRN_HD),
                         lambda i: (jnp.minimum(seq_idx(i), N_PROMPT_SEQ - 1), 0, 0, 0)),
        ],
        out_shape=[
            jax.ShapeDtypeStruct((HGRN_HEADS, NT_TOK, HGRN_HD), BF16),
            jax.ShapeDtypeStruct((N_PROMPT_SEQ, HGRN_HEADS, HGRN_HD, HGRN_HD), F32),
        ],
        scratch_shapes=[pltpu.VMEM((HGRN_HEADS, HGRN_HD, HGRN_HD), F32)],
        compiler_params=_cparams(("arbitrary",)),
        name="hgrn_scan_bwd" if rev else "hgrn_scan_fwd",
    )(p_hm, p_hm, p_hm, lb_d, s0_d)


def _hgrn_out_kernel(*refs):
    of_ref, ob_ref, zg_ref, ng_ref, *y_refs, g1_ref, gate_ref, w_ref, o_ref, x_ref = refs
    ng = ng_ref[...]
    for h in range(HGRN_HEADS):
        o = of_ref[h].astype(F32) + ob_ref[h].astype(F32)
        x_ref[:, h * HGRN_HD:(h + 1) * HGRN_HD] = (_rms(o, ng) * _silu(zg_ref[0, h].astype(F32))).astype(BF16)
    acc = _dot(x_ref[...], w_ref[0])
    o_ref[...] = _stream_tile(y_refs) + gate_ref[0, 0] * _rms(acc, g1_ref[0])


def _hgrn_out(o_f, o_b, p_hm, hgrn_norm_g, ys, norm_l, mods_l, w, j):
    head_spec = pl.BlockSpec((HGRN_HEADS, TM, HGRN_HD), lambda i: (0, i, 0))
    return pl.pallas_call(
        _hgrn_out_kernel,
        grid=(NT_TOK // TM,),
        in_specs=[
            head_spec, head_spec,
            pl.BlockSpec((1, HGRN_HEADS, TM, HGRN_HD), lambda i: (4, 0, i, 0)),
            pl.BlockSpec((1, HGRN_HD), lambda i: (0, 0))] + _stream_specs(len(ys) == 2) + [
            _norm_spec(1), _mod_spec(2),
            pl.BlockSpec((1, D_MODEL, D_MODEL), lambda i: (j, 0, 0), pipeline_mode=pl.Buffered(1)),
        ],
        out_specs=_tok_spec(D_MODEL),
        out_shape=jax.ShapeDtypeStruct((NT_TOK, D_MODEL), F32),
        scratch_shapes=[pltpu.VMEM((TM, D_MODEL), BF16)],
        compiler_params=_cparams(("parallel",)),
        name="hgrn_out",
    )(o_f, o_b, p_hm, hgrn_norm_g, *ys, norm_l, mods_l, w)


def _mla_down_kernel(y_ref, g0_ref, sh_ref, sc_ref, w_ref, qg_ref, kvg_ref, cos_ref, sin_ref,
                     cq_ref, ckv_ref, kpe_ref, kpe128_ref):
    p = _dot(_prenorm(y_ref, g0_ref, sh_ref, sc_ref), w_ref[...])
    cq_ref[...] = _rms(p[:, :MLA_Q_LORA], qg_ref[...]).astype(BF16)
    ckv_ref[...] = _rms(p[:, MLA_Q_LORA:MLA_Q_LORA + MLA_KV_LORA], kvg_ref[...])
    tail = p[:, MLA_Q_LORA + MLA_KV_LORA:]
    kpe_ref[...] = tail[:, :MLA_ROPE]
    kpe128_ref[...] = _rope128(tail, cos_ref[...], sin_ref[...]).astype(BF16)


def _mla_down(y, norm_l, mods_l, w, qg, kvg, cos_t, sin_t):
    return pl.pallas_call(
        _mla_down_kernel,
        grid=(NT_TOK // TM,),
        in_specs=[_tok_spec(D_MODEL), _norm_spec(0), _mod_spec(0), _mod_spec(1),
                  _resident((D_MODEL, MLA_DOWN_PAD)),
                  pl.BlockSpec((1, MLA_Q_LORA), lambda i: (0, 0)),
                  pl.BlockSpec((1, MLA_KV_LORA), lambda i: (0, 0)),
                  _rope_spec(), _rope_spec()],
        out_specs=[_tok_spec(MLA_Q_LORA), _tok_spec(MLA_KV_LORA), _tok_spec(MLA_ROPE), _tok_spec(128)],
        out_shape=[
            jax.ShapeDtypeStruct((NT_TOK, MLA_Q_LORA), BF16),
            jax.ShapeDtypeStruct((NT_TOK, MLA_KV_LORA), F32),
            jax.ShapeDtypeStruct((NT_TOK, MLA_ROPE), F32),
            jax.ShapeDtypeStruct((NT_TOK, 128), BF16),
        ],
        compiler_params=_cparams(("parallel",)),
        name="mla_down",
    )(y, norm_l, mods_l, mods_l, w, qg, kvg, cos_t, sin_t)


MLA_HB = 8


def _mla_q_kernel(cq_ref, w_ref, cos_ref, sin_ref, q_ref):
    acc = _dot(cq_ref[...], w_ref[0]) * MLA_Q_SCALE
    cos = cos_ref[...]
    sin = sin_ref[...]
    for hh in range(MLA_HB):
        c0 = hh * MLA_QK_PAD
        q_ref[hh, :, :MLA_NOPE] = acc[:, c0:c0 + MLA_NOPE].astype(BF16)
        q_ref[hh, :, MLA_NOPE:] = _rope128(acc[:, c0 + MLA_NOPE:c0 + MLA_QK_PAD], cos, sin).astype(BF16)


def _mla_q(cq, w_uq_g, cos_t, sin_t):
    return pl.pallas_call(
        _mla_q_kernel,
        grid=(NT_TOK // TM, MLA_HEADS // MLA_HB),
        in_specs=[_tok_spec(MLA_Q_LORA),
                  pl.BlockSpec((1, MLA_Q_LORA, MLA_HB * MLA_QK_PAD), lambda i, h: (h, 0, 0)),
                  _rope_spec(), _rope_spec()],
        out_specs=pl.BlockSpec((MLA_HB, TM, MLA_QK_PAD), lambda i, h: (h, i, 0)),
        out_shape=jax.ShapeDtypeStruct((MLA_HEADS, NT_TOK, MLA_QK_PAD), BF16),
        compiler_params=_cparams(("parallel", "arbitrary")),
        name="mla_q",
    )(cq, w_uq_g, cos_t, sin_t)


def _mla_kv_kernel(ckv_ref, kpe_ref, w_ref, k_ref, vt_ref):
    acc = _dot(ckv_ref[...], w_ref[0])
    kpe = kpe_ref[...]
    for hh in range(MLA_HB):
        c0 = hh * (MLA_NOPE + MLA_V)
        k_ref[hh, :, :MLA_NOPE] = acc[:, c0:c0 + MLA_NOPE].astype(BF16)
        k_ref[hh, :, MLA_NOPE:] = kpe
        vt_ref[hh] = acc[:, c0 + MLA_NOPE:c0 + MLA_NOPE + MLA_V].T.astype(BF16)


def _mla_kv(ckv_all, kpe_all, w_ukv_g):
    n = ckv_all.shape[0]
    return pl.pallas_call(
        _mla_kv_kernel,
        grid=(n // TM, MLA_HEADS // MLA_HB),
        in_specs=[_tok_spec(MLA_KV_LORA), _tok_spec(128),
                  pl.BlockSpec((1, MLA_KV_LORA, MLA_HB * (MLA_NOPE + MLA_V)), lambda i, h: (h, 0, 0))],
        out_specs=[pl.BlockSpec((MLA_HB, TM, MLA_QK_PAD), lambda i, h: (h, i, 0)),
                   pl.BlockSpec((MLA_HB, MLA_V, TM), lambda i, h: (h, 0, i))],
        out_shape=[jax.ShapeDtypeStruct((MLA_HEADS, n, MLA_QK_PAD), BF16),
                   jax.ShapeDtypeStruct((MLA_HEADS, MLA_V, n), BF16)],
        compiler_params=_cparams(("parallel", "arbitrary")),
        name="mla_kv",
    )(ckv_all, kpe_all, w_ukv_g)


def _row_partial(x, reduce_fn, combine_fn):
    rows, n = x.shape
    x3 = x.reshape(rows // 8, 8, n)
    parts = min(ROW_REDUCE_PARTS, rows // 8)
    bounds = [i * (rows // 8) // parts for i in range(parts + 1)]
    acc = [reduce_fn(x3[lo:hi], axis=0) for lo, hi in zip(bounds[:-1], bounds[1:])]
    while len(acc) > 1:
        acc = [combine_fn(a, b) for a, b in zip(acc[0::2], acc[1::2])] + ([acc[-1]] if len(acc) % 2 else [])
    return acc[0]


MLA_PROMPT_HB = 4


def _mla_attn_prompt_kernel(q_ref, k_ref, vt_ref, o_ref):
    for u in range(MLA_PROMPT_HB):
        s = _dot_nt(k_ref[u], q_ref[u])
        m = jnp.max(_row_partial(s, jnp.max, jnp.maximum), axis=0, keepdims=True)
        p = jnp.exp2(s - m)
        den = jnp.sum(_row_partial(p, jnp.sum, jnp.add), axis=0, keepdims=True)
        acc = _dot(vt_ref[u], p.astype(BF16))
        o_ref[:, u * MLA_V:(u + 1) * MLA_V] = (acc / den).T.astype(BF16)


MLA_TQ = 512
ROW_REDUCE_PARTS = 4


def _mla_attn_pipe_kernel(q_ref, k_ref, vt_ref, o_ref, s_ref):
    tq = MLA_TQ
    nt = SAMPLE_LEN // tq
    chunks = list(zip(MLA_KEY_CHUNKS[:-1], MLA_KEY_CHUNKS[1:]))

    def tile_rows(t):
        return pl.ds(t * tq, tq) if isinstance(t, int) else pl.ds(pl.multiple_of(t * tq, tq), tq)

    def stage(t_a, slot_a, t_b, slot_b, m_b):
        if t_a is not None:
            q = q_ref[0, tile_rows(t_a), :]
        m_parts = []
        den = acc = None
        for lo, hi in chunks:
            if t_a is not None:
                s = _dot_nt(k_ref[0, lo:hi, :], q)
                s_ref[slot_a, lo:hi, :] = s
                m_parts.append(_row_partial(s, jnp.max, jnp.maximum))
            if t_b is not None:
                p = jnp.exp2(s_ref[slot_b, lo:hi, :] - m_b)
                d = _row_partial(p, jnp.sum, jnp.add)
                a = _dot(vt_ref[0, :, lo:hi], p.astype(BF16))
                den = d if den is None else den + d
                acc = a if acc is None else acc + a
        if t_b is not None:
            den = jnp.sum(den, axis=0, keepdims=True)
            o_ref[tile_rows(t_b), :] = (acc / den).T.astype(BF16)
        if t_a is None:
            return None
        m_a = m_parts[0]
        for part in m_parts[1:]:
            m_a = jnp.maximum(m_a, part)
        return jnp.max(m_a, axis=0, keepdims=True)

    m = stage(0, 0, None, None, None)

    def body(j, m_prev):
        t = 2 * j + 1
        m_mid = stage(t, 1, t - 1, 0, m_prev)
        return stage(t + 1, 0, t, 1, m_mid)

    m = lax.fori_loop(0, (nt - 2) // 2, body, m)
    m = stage(nt - 1, 1, nt - 2, 0, m)
    stage(None, None, nt - 1, 1, m)


def _mla_attn(q_hm, k_hm, vt_hm):
    pk0 = N_SAMPLE_SEQ * MLA_KV_LEN // PROMPT_LEN
    hb = MLA_PROMPT_HB
    o_p = pl.pallas_call(
        _mla_attn_prompt_kernel,
        grid=(MLA_HEADS // hb, N_PROMPT_SEQ),
        in_specs=[pl.BlockSpec((hb, PROMPT_LEN, MLA_QK_PAD), lambda h, b: (h, b, 0)),
                  pl.BlockSpec((hb, PROMPT_LEN, MLA_QK_PAD), lambda h, b: (h, pk0 + b, 0)),
                  pl.BlockSpec((hb, MLA_V, PROMPT_LEN), lambda h, b: (h, 0, pk0 + b))],
        out_specs=pl.BlockSpec((PROMPT_LEN, hb * MLA_V), lambda h, b: (b, h)),
        out_shape=jax.ShapeDtypeStruct((NP_TOK, MLA_HEADS * MLA_V), BF16),
        compiler_params=_cparams(("parallel", "parallel")),
        name="mla_attn_prompt",
    )(q_hm, k_hm, vt_hm)
    q0 = NP_TOK // SAMPLE_LEN
    o_s = pl.pallas_call(
        _mla_attn_pipe_kernel,
        grid=(MLA_HEADS, N_SAMPLE_SEQ),
        in_specs=[pl.BlockSpec((1, SAMPLE_LEN, MLA_QK_PAD), lambda h, b: (h, q0 + b, 0)),
                  pl.BlockSpec((1, MLA_KV_LEN, MLA_QK_PAD), lambda h, b: (h, b, 0)),
                  pl.BlockSpec((1, MLA_V, MLA_KV_LEN), lambda h, b: (h, 0, b))],
        out_specs=pl.BlockSpec((SAMPLE_LEN, MLA_V), lambda h, b: (b, h)),
        out_shape=jax.ShapeDtypeStruct((NS_TOK, MLA_HEADS * MLA_V), BF16),
        scratch_shapes=[pltpu.VMEM((2, MLA_KV_LEN, MLA_TQ), F32)],
        compiler_params=_cparams(("parallel", "parallel")),
        name="mla_attn_sample",
    )(q_hm, k_hm, vt_hm)
    return o_p, o_s


def _swa_qkv_kernel(y_ref, g0_ref, sh_ref, sc_ref, w_ref, cos_ref, sin_ref, q_ref, kv_ref, kf_ref, vf_ref):
    p = _dot(_prenorm(y_ref, g0_ref, sh_ref, sc_ref), w_ref[...])
    cos = cos_ref[...]
    sin = sin_ref[...]
    kf_ref[...] = p[:, SWA_QW:SWA_QW + SWA_KVW]
    vf_ref[...] = p[:, SWA_QW + SWA_KVW:]
    for c in range(SWA_QW // 128):
        r = _rope128(p[:, c * 128:(c + 1) * 128], cos, sin).astype(BF16)
        q_ref[2 * c] = r[:, :SWA_HD]
        q_ref[2 * c + 1] = r[:, SWA_HD:]
    for c in range(SWA_KVW // 128):
        k0 = SWA_QW + c * 128
        v0 = SWA_QW + SWA_KVW + c * 128
        k = _rope128(p[:, k0:k0 + 128], cos, sin).astype(BF16)
        v = p[:, v0:v0 + 128].astype(BF16)
        kv_ref[2 * c] = jnp.concatenate([k[:, :SWA_HD], v[:, :SWA_HD]], axis=1)
        kv_ref[2 * c + 1] = jnp.concatenate([k[:, SWA_HD:], v[:, SWA_HD:]], axis=1)


def _swa_qkv(y, norm_l, mods_l, w, cos_t, sin_t):
    return pl.pallas_call(
        _swa_qkv_kernel,
        grid=(NT_TOK // TM,),
        in_specs=[_tok_spec(D_MODEL), _norm_spec(0), _mod_spec(0), _mod_spec(1),
                  _resident((D_MODEL, SWA_QW + 2 * SWA_KVW)), _rope_spec(), _rope_spec()],
        out_specs=[pl.BlockSpec((SWA_Q_HEADS, TM, SWA_HD), lambda i: (0, i, 0)),
                   pl.BlockSpec((SWA_KV_HEADS, TM, 2 * SWA_HD), lambda i: (0, i, 0)),
                   _tok_spec(SWA_KVW), _tok_spec(SWA_KVW)],
        out_shape=[
            jax.ShapeDtypeStruct((SWA_Q_HEADS, NT_TOK, SWA_HD), BF16),
            jax.ShapeDtypeStruct((SWA_KV_HEADS, NT_TOK, 2 * SWA_HD), BF16),
            jax.ShapeDtypeStruct((NT_TOK, SWA_KVW), F32),
            jax.ShapeDtypeStruct((NT_TOK, SWA_KVW), F32),
        ],
        compiler_params=_cparams(("parallel",)),
        name="swa_qkv",
    )(y, norm_l, mods_l, mods_l, w, cos_t, sin_t)


def _swa_attn_kernel(sink_ref, q_ref, kvc_ref, *rest, local):
    o_ref = rest[-1]
    kv0 = pl.program_id(0) * SWA_KVB
    tq = q_ref.shape[1]
    nq = SWA_GROUP * tq
    kvc = kvc_ref[...].reshape(SWA_KVB, kvc_ref.shape[-2], 2 * SWA_HD)
    head = lax.broadcasted_iota(jnp.int32, (1, nq), 1) >> (tq.bit_length() - 1)
    if local:
        j = pl.program_id(2)
        nb = pl.num_programs(2)
        key = lax.broadcasted_iota(jnp.int32, (SWA_BLOCK, nq), 0)
        qry = lax.broadcasted_iota(jnp.int32, (SWA_BLOCK, nq), 1) & (SWA_BLOCK - 1)
        keep = [jnp.logical_and(key >= qry, j > 0), None, jnp.logical_and(key <= qry, j < nb - 1)]
    for u in range(SWA_KVB):
        q = q_ref[u * SWA_GROUP:(u + 1) * SWA_GROUP].reshape(nq, SWA_HD) * (SWA_HD ** -0.5)
        sink = jnp.zeros((1, nq), F32)
        for g in range(SWA_GROUP):
            sink = jnp.where(head == g, sink_ref[(kv0 + u) * SWA_GROUP + g], sink)
        blocks = [kvc[u]]
        scores = [_dot_nt(blocks[0][:, :SWA_HD], q)]
        if local:
            for t in range(3):
                blk = rest[t][u]
                s = _dot_nt(blk[:, :SWA_HD], q)
                scores.append(s if keep[t] is None else jnp.where(keep[t], s, NEG_INF))
                blocks.append(blk)
        s = scores[0] if len(scores) == 1 else jnp.concatenate(scores, axis=0)
        kv_all = blocks[0] if len(blocks) == 1 else jnp.concatenate(blocks, axis=0)
        m = jnp.maximum(jnp.max(_row_partial(s, jnp.max, jnp.maximum), axis=0, keepdims=True), sink)
        p = jnp.exp(s - m)
        den = jnp.sum(_row_partial(p, jnp.sum, jnp.add), axis=0, keepdims=True) + jnp.exp(sink - m)
        kv_t = kv_all.astype(F32).T.astype(BF16)
        out = (_dot(kv_t, p.astype(BF16)) / den).T.astype(BF16)
        for g in range(SWA_GROUP):
            c0 = (u * SWA_GROUP + g) * SWA_HD
            o_ref[:, c0:c0 + SWA_HD] = out[g * tq:(g + 1) * tq, SWA_HD:]


def _swa_attn(sink, q_hm, kv_hm, kvc_s):
    smem = pl.BlockSpec(memory_space=pltpu.SMEM)
    gw = SWA_KVB * SWA_GROUP * SWA_HD
    qb = SWA_KVB * SWA_GROUP
    o_p = pl.pallas_call(
        functools.partial(_swa_attn_kernel, local=False),
        grid=(SWA_KV_HEADS // SWA_KVB, N_PROMPT_SEQ),
        in_specs=[smem,
                  pl.BlockSpec((qb, PROMPT_LEN, SWA_HD), lambda kv, b: (kv, b, 0)),
                  pl.BlockSpec((SWA_KVB, PROMPT_LEN, 2 * SWA_HD), lambda kv, b: (kv, b, 0))],
        out_specs=pl.BlockSpec((PROMPT_LEN, gw), lambda kv, b: (b, kv)),
        out_shape=jax.ShapeDtypeStruct((NP_TOK, SWA_QW), BF16),
        compiler_params=_cparams(("parallel", "parallel")),
        name="swa_attn_prompt",
    )(sink, q_hm, kv_hm)
    nb = SAMPLE_LEN // SWA_BLOCK
    b0 = NP_TOK // SWA_BLOCK

    def loc(off):
        return pl.BlockSpec((SWA_KVB, SWA_BLOCK, 2 * SWA_HD),
                            lambda kv, b, j: (kv, b0 + b * nb + jnp.clip(j + off, 0, nb - 1), 0))

    o_s = pl.pallas_call(
        functools.partial(_swa_attn_kernel, local=True),
        grid=(SWA_KV_HEADS // SWA_KVB, N_SAMPLE_SEQ, nb),
        in_specs=[smem,
                  pl.BlockSpec((qb, SWA_BLOCK, SWA_HD), lambda kv, b, j: (kv, b0 + b * nb + j, 0)),
                  pl.BlockSpec((1, SWA_KVB, PAST_LEN, 2 * SWA_HD), lambda kv, b, j: (b, kv, 0, 0)),
                  loc(-1), loc(0), loc(1)],
        out_specs=pl.BlockSpec((SWA_BLOCK, gw), lambda kv, b, j: (b * nb + j, kv)),
        out_shape=jax.ShapeDtypeStruct((NS_TOK, SWA_QW), BF16),
        compiler_params=_cparams(("parallel", "parallel", "arbitrary")),
        name="swa_attn_sample",
    )(sink, q_hm, kvc_s, kv_hm, kv_hm, kv_hm)
    return o_p, o_s


def _rope_tables():
    t = jnp.arange(SAMPLE_LEN, dtype=jnp.int32)
    inv = ROPE_BASE ** (-jnp.arange(16, dtype=F32) / 16)
    ang_r = (t // GRID_W).astype(F32)[:, None] * inv[None, :]
    ang_c = (t % GRID_W).astype(F32)[:, None] * inv[None, :]
    cos = jnp.concatenate([jnp.cos(ang_r), jnp.cos(ang_r), jnp.cos(ang_c), jnp.cos(ang_c)], axis=-1)
    sin = jnp.concatenate([-jnp.sin(ang_r), jnp.sin(ang_r), -jnp.sin(ang_c), jnp.sin(ang_c)], axis=-1)
    cos = jnp.concatenate([cos, cos], axis=-1)
    sin = jnp.concatenate([sin, sin], axis=-1)
    cos_t = jnp.concatenate([jnp.ones((NP_TOK, 128), F32), cos], axis=0)
    sin_t = jnp.concatenate([jnp.zeros((NP_TOK, 128), F32), sin], axis=0)
    return cos_t, sin_t


def kernel(x_prompt, x_sample, c, state_hgrn, cache_mla_ckv, cache_mla_kpe, cache_swa_k, cache_swa_v, c_ctx, ada_w, ada_b, norm_g, mlp_w_in, mlp_w_out, hgrn_w_in, hgrn_lb_logits, hgrn_norm_g, hgrn_w_out, mla_w_down, mla_q_norm_g, mla_kv_norm_g, mla_w_uq, mla_w_ukv, mla_w_out, swa_w_qkv, swa_sink, swa_w_out):
    ys = (x_prompt.reshape(NP_TOK, D_MODEL), x_sample.reshape(NS_TOK, D_MODEL))
    cond8 = jnp.concatenate([c_ctx[None, :], c, jnp.zeros((3, D_MODEL), F32)], axis=0)
    mods = _adaln(cond8, ada_w, ada_b).reshape(DEPTH, 8, N_MOD, 1, D_MODEL)
    cos_t, sin_t = _rope_tables()

    lb_soft = jax.nn.softmax(hgrn_lb_logits.astype(F32), axis=1)
    lb_all = jnp.cumsum(lb_soft, axis=1) - lb_soft[:, :1]

    w1_all = mlp_w_in.astype(BF16)
    w2_all = mlp_w_out.astype(BF16)
    hgrn_w_in_all = hgrn_w_in.astype(BF16)
    hgrn_w_out_all = hgrn_w_out.astype(BF16)

    new_hgrn = []
    new_ckv = new_kpe = new_k = new_v = None
    for layer in range(DEPTH):
        kind, j = layer % 3, layer // 3
        norm_l = norm_g[layer].reshape(4, 1, D_MODEL)
        mods_l = mods[layer]
        if kind == 0:
            if layer > 0:
                ys = (y,)
            p_hm = _hgrn_in(ys, norm_l, mods_l, hgrn_w_in_all, j)
            lb = lb_all[:, layer].reshape(2, HGRN_HEADS, 1, HGRN_HD)
            o_f, s_f = _hgrn_scan(p_hm, lb[0], state_hgrn[:, j, 0], rev=False)
            o_b, s_b = _hgrn_scan(p_hm, lb[1], state_hgrn[:, j, 1], rev=True)
            new_hgrn.append(jnp.stack([s_f, s_b], axis=1))
            y = _hgrn_out(o_f, o_b, p_hm, hgrn_norm_g[j].reshape(1, HGRN_HD), ys, norm_l, mods_l,
                          hgrn_w_out_all, j)
        elif kind == 1:
            w_down = jnp.pad(mla_w_down[j], ((0, 0), (0, MLA_DOWN_PAD - mla_w_down.shape[-1]))).astype(BF16)
            cq, ckv, kpe, kpe128 = _mla_down(y, norm_l, mods_l, w_down, mla_q_norm_g[j].reshape(1, -1),
                                             mla_kv_norm_g[j].reshape(1, -1), cos_t, sin_t)
            ng = MLA_HEADS // MLA_HB
            w_uq = jnp.pad(mla_w_uq[j].reshape(MLA_Q_LORA, MLA_HEADS, MLA_QK),
                           ((0, 0), (0, 0), (0, MLA_QK_PAD - MLA_QK)))
            w_uq = w_uq.reshape(MLA_Q_LORA, ng, MLA_HB * MLA_QK_PAD).transpose(1, 0, 2).astype(BF16)
            w_ukv = mla_w_ukv[j].reshape(MLA_KV_LORA, ng, MLA_HB * (MLA_NOPE + MLA_V)).transpose(1, 0, 2).astype(BF16)
            q_hm = _mla_q(cq, w_uq, cos_t, sin_t)
            ckv_s = jnp.concatenate([cache_mla_ckv[:, j].astype(BF16),
                                     ckv[NP_TOK:].reshape(N_SAMPLE_SEQ, SAMPLE_LEN, MLA_KV_LORA).astype(BF16)], axis=1)
            kpe_ctx = jnp.pad(cache_mla_kpe[:, j], ((0, 0), (0, 0), (0, 128 - MLA_ROPE))).astype(BF16)
            kpe_s = jnp.concatenate([kpe_ctx, kpe128[NP_TOK:].reshape(N_SAMPLE_SEQ, SAMPLE_LEN, 128)], axis=1)
            ckv_all = jnp.concatenate([ckv_s.reshape(-1, MLA_KV_LORA), ckv[:NP_TOK].astype(BF16)], axis=0)
            kpe_all = jnp.concatenate([kpe_s.reshape(-1, 128), kpe128[:NP_TOK]], axis=0)
            k_hm, v_hm = _mla_kv(ckv_all, kpe_all, w_ukv)
            o_p, o_s = _mla_attn(q_hm, k_hm, v_hm)
            new_ckv = ckv[:NP_TOK].reshape(N_PROMPT_SEQ, 1, PROMPT_LEN, MLA_KV_LORA)
            new_kpe = kpe[:NP_TOK].reshape(N_PROMPT_SEQ, 1, PROMPT_LEN, MLA_ROPE)
            y = _outproj(o_p, o_s, y, norm_l, mods_l, mla_w_out[j].astype(BF16))
        else:
            q_hm, kv_hm, kf, vf = _swa_qkv(y, norm_l, mods_l, swa_w_qkv[j].astype(BF16), cos_t, sin_t)
            kvc_s = jnp.concatenate([cache_swa_k[:, j], cache_swa_v[:, j]], axis=-1).transpose(0, 2, 1, 3).astype(BF16)
            o_p, o_s = _swa_attn(swa_sink[j], q_hm, kv_hm, kvc_s)
            new_k = kf[:NP_TOK].reshape(N_PROMPT_SEQ, 1, PROMPT_LEN, SWA_KV_HEADS, SWA_HD)
            new_v = vf[:NP_TOK].reshape(N_PROMPT_SEQ, 1, PROMPT_LEN, SWA_KV_HEADS, SWA_HD)
            y = _outproj(o_p, o_s, y, norm_l, mods_l, swa_w_out[j].astype(BF16))
        if layer < DEPTH - 1:
            y = _mlp(y, norm_l, mods_l, w1_all, w2_all, layer, 0, NT_TOK // TM)
        else:
            y_prompt = _mlp(y, norm_l, mods_l, w1_all, w2_all, layer, 0, PROMPT_TILES)
            y_sample = _mlp(y, norm_l, mods_l, w1_all, w2_all, layer, PROMPT_TILES, NS_TOK // TM)

    return (y_prompt.reshape(N_PROMPT_SEQ, PROMPT_LEN, D_MODEL), y_sample.reshape(N_SAMPLE_SEQ, SAMPLE_LEN, D_MODEL),
            jnp.stack(new_hgrn, axis=1), new_ckv, new_kpe, new_k, new_v)
```

```python
import functools

import jax
import jax.numpy as jnp
from jax import lax
from jax.experimental import pallas as pl
from jax.experimental.pallas import tpu as pltpu

F32 = jnp.float32
BF16 = jnp.bfloat16

D_MODEL = 2048
DEPTH = 4
N_PROMPT_SEQ = 16
PROMPT_LEN = 256
N_SAMPLE_SEQ = 4
SAMPLE_LEN = 4096
PAST_LEN = 256
NP_TOK = N_PROMPT_SEQ * PROMPT_LEN
NS_TOK = N_SAMPLE_SEQ * SAMPLE_LEN
NT_TOK = NP_TOK + NS_TOK
GROUP_TOK = 4096
N_GROUPS = NT_TOK // GROUP_TOK
N_MOD = 6
D_FF = 4 * D_MODEL
NORM_EPS = 1e-6
GRID_W = 64
ROPE_BASE = 10000.0

HGRN_HEADS = 16
HGRN_HD = 128
HGRN_CHUNK = 128
HGRN_BLOCK = 256
SUBLANES = 8
HGRN_DIAG = 8
HGRN_UNROLL = 16

MLA_HEADS = 16
MLA_Q_LORA = 512
MLA_KV_LORA = 512
MLA_NOPE = 128
MLA_ROPE = 64
MLA_V = 128
MLA_QK = MLA_NOPE + MLA_ROPE
MLA_QK_PAD = 256
MLA_DOWN_PAD = 1152
MLA_KV_LEN = PAST_LEN + SAMPLE_LEN
MLA_KEY_CHUNKS = (0, 640, 1280, 1792, 2304, 2816, 3328, 3840, MLA_KV_LEN)
MLA_Q_SCALE = MLA_QK ** -0.5 * 1.4426950408889634

SWA_HD = 64
SWA_Q_HEADS = 32
SWA_KV_HEADS = 8
SWA_GROUP = SWA_Q_HEADS // SWA_KV_HEADS
SWA_QW = SWA_Q_HEADS * SWA_HD
SWA_KVW = SWA_KV_HEADS * SWA_HD
SWA_BLOCK = 128
SWA_KVB = 8
NEG_INF = -1e30

TM = 512
TILES_PER_GROUP = GROUP_TOK // TM
VMEM_LIMIT = 56 * 1024 * 1024


def _cparams(sem):
    return pltpu.CompilerParams(dimension_semantics=sem, vmem_limit_bytes=VMEM_LIMIT)


def _dot(a, b):
    return jnp.dot(a, b, preferred_element_type=F32)


def _dot_nt(a, b):
    return lax.dot_general(a, b, (((1,), (1,)), ((), ())), preferred_element_type=F32)


def _rms(x, g):
    return x * lax.rsqrt(jnp.mean(x * x, axis=-1, keepdims=True) + NORM_EPS) * g


def _silu(x):
    return x * jax.nn.sigmoid(x)


def _prenorm(y_ref, g_ref, sh_ref, sc_ref):
    return (_rms(y_ref[...], g_ref[0]) * (1.0 + sc_ref[0, 0]) + sh_ref[0, 0]).astype(BF16)


def _post(y_ref, acc, g_ref, gate_ref):
    return y_ref[...] + gate_ref[0, 0] * _rms(acc, g_ref[0])


def _rope128(x, cos, sin):
    lane = lax.broadcasted_iota(jnp.int32, x.shape, 1)
    first = (lane & 31) < 16
    swapped = jnp.where(first, pltpu.roll(x, 112, 1), pltpu.roll(x, 16, 1))
    return x * cos + swapped * sin


def _tok_spec(n):
    return pl.BlockSpec((TM, n), lambda i, *_: (i, 0))


def _norm_spec(k):
    return pl.BlockSpec((1, 1, D_MODEL), lambda i, *_: (k, 0, 0))


def _mod_spec(k):
    return pl.BlockSpec((1, 1, 1, D_MODEL), lambda i, *_: (i // TILES_PER_GROUP, k, 0, 0))


def _rope_spec():
    def idx(i, *_):
        return (jnp.where(i < TILES_PER_GROUP, i, TILES_PER_GROUP + i % TILES_PER_GROUP), 0)
    return pl.BlockSpec((TM, 128), idx)


def _resident(shape):
    nd = len(shape)
    return pl.BlockSpec(shape, lambda *_: (0,) * nd, pipeline_mode=pl.Buffered(1))


def _adaln_kernel(c_ref, w_ref, b_ref, o_ref):
    s = _silu(c_ref[...])
    s_hi = s.astype(BF16)
    s_lo = (s - s_hi.astype(F32)).astype(BF16)
    w = w_ref[0]
    w_hi = w.astype(BF16)
    w_lo = (w - w_hi.astype(F32)).astype(BF16)
    o_ref[0] = _dot(s_hi, w_hi) + _dot(s_lo, w_hi) + _dot(s_hi, w_lo) + b_ref[0]


def _adaln(cond8, ada_w, ada_b):
    tn = 1024
    n = N_MOD * D_MODEL
    return pl.pallas_call(
        _adaln_kernel,
        grid=(DEPTH, n // tn),
        in_specs=[
            pl.BlockSpec((8, D_MODEL), lambda l, j: (0, 0)),
            pl.BlockSpec((1, D_MODEL, tn), lambda l, j: (l, 0, j)),
            pl.BlockSpec((1, 1, tn), lambda l, j: (l, 0, j)),
        ],
        out_specs=pl.BlockSpec((1, 8, tn), lambda l, j: (l, 0, j)),
        out_shape=jax.ShapeDtypeStruct((DEPTH, 8, n), F32),
        compiler_params=_cparams(("parallel", "parallel")),
        name="adaln",
    )(cond8, ada_w, ada_b.reshape(DEPTH, 1, n))


def _prenorm_rows(y_ref, rows, g_ref, sh_ref, sc_ref):
    return (_rms(y_ref[rows, :], g_ref[0]) * (1.0 + sc_ref[0, 0]) + sh_ref[0, 0]).astype(BF16)


def _mlp_kernel(y_ref, yn_ref, g2_ref, g3_ref, sh_ref, sc_ref, shn_ref, scn_ref, gate_ref, w1_ref, w2_ref,
                o_ref, h_ref, *, nf):
    i = pl.program_id(0)
    f = pl.program_id(1)
    cur = i % 2
    all_rows = pl.ds(0, TM)

    @pl.when(jnp.logical_and(i == 0, f == 0))
    def _():
        h_ref[0] = _prenorm_rows(y_ref, all_rows, g2_ref, sh_ref, sc_ref)

    def partial_sum():
        u = jnp.maximum(_dot(h_ref[cur], w1_ref[0]), 0.0)
        part = _dot((u * u).astype(BF16), w2_ref[0])
        rows = pl.ds(pl.multiple_of(f * (TM // nf), TM // nf), TM // nf)
        h_ref[1 - cur, rows, :] = _prenorm_rows(yn_ref, rows, g2_ref, shn_ref, scn_ref)
        return part

    @pl.when(f == 0)
    def _():
        o_ref[...] = partial_sum()

    @pl.when(jnp.logical_and(f > 0, f < nf - 1))
    def _():
        o_ref[...] += partial_sum()

    @pl.when(f == nf - 1)
    def _():
        o_ref[...] = _post(y_ref, o_ref[...] + partial_sum(), g3_ref, gate_ref)


def _mlp(y, norm_l, mods_l, w1, w2, layer, tile0, ntiles):
    tf = 1024

    def nxt(i):
        return jnp.minimum(i + 1, ntiles - 1) + tile0

    def mod(k, tile_of):
        return pl.BlockSpec((1, 1, 1, D_MODEL), lambda i, f: (tile_of(i) // TILES_PER_GROUP, k, 0, 0))

    def cur(i):
        return i + tile0

    return pl.pallas_call(
        functools.partial(_mlp_kernel, nf=D_FF // tf),
        grid=(ntiles, D_FF // tf),
        in_specs=[
            pl.BlockSpec((TM, D_MODEL), lambda i, f: (cur(i), 0)),
            pl.BlockSpec((TM, D_MODEL), lambda i, f: (nxt(i), 0)),
            _norm_spec(2), _norm_spec(3),
            mod(3, cur), mod(4, cur), mod(3, nxt), mod(4, nxt), mod(5, cur),
            pl.BlockSpec((1, D_MODEL, tf), lambda i, f: (layer, 0, f)),
            pl.BlockSpec((1, tf, D_MODEL), lambda i, f: (layer, f, 0)),
        ],
        out_specs=_tok_spec(D_MODEL),
        out_shape=jax.ShapeDtypeStruct((ntiles * TM, D_MODEL), F32),
        scratch_shapes=[pltpu.VMEM((2, TM, D_MODEL), BF16)],
        compiler_params=_cparams(("arbitrary", "arbitrary")),
        name="mlp",
    )(y, y, norm_l, norm_l, mods_l, mods_l, mods_l, mods_l, mods_l, w1, w2)


PROMPT_TILES = NP_TOK // TM


def _outproj_kernel(xp_ref, xs_ref, y_ref, g1_ref, gate_ref, w_ref, o_ref):
    x = jnp.where(pl.program_id(0) < PROMPT_TILES, xp_ref[...], xs_ref[...])
    o_ref[...] = _post(y_ref, _dot(x, w_ref[...]), g1_ref, gate_ref)


def _outproj(x_p, x_s, y, norm_l, mods_l, w):
    return pl.pallas_call(
        _outproj_kernel,
        grid=(NT_TOK // TM,),
        in_specs=[pl.BlockSpec((TM, D_MODEL), lambda i: (jnp.minimum(i, PROMPT_TILES - 1), 0)),
                  pl.BlockSpec((TM, D_MODEL), lambda i: (jnp.maximum(i - PROMPT_TILES, 0), 0)),
                  _tok_spec(D_MODEL), _norm_spec(1), _mod_spec(2),
                  _resident((D_MODEL, D_MODEL))],
        out_specs=_tok_spec(D_MODEL),
        out_shape=jax.ShapeDtypeStruct((NT_TOK, D_MODEL), F32),
        compiler_params=_cparams(("parallel",)),
        name="outproj",
    )(x_p, x_s, y, norm_l, mods_l, w)


def _stream_specs(split):
    if not split:
        return [_tok_spec(D_MODEL)]
    return [pl.BlockSpec((TM, D_MODEL), lambda i, *_: (jnp.minimum(i, PROMPT_TILES - 1), 0)),
            pl.BlockSpec((TM, D_MODEL), lambda i, *_: (jnp.maximum(i - PROMPT_TILES, 0), 0))]


def _stream_tile(refs):
    if len(refs) == 1:
        return refs[0][...]
    return jnp.where(pl.program_id(0) < PROMPT_TILES, refs[0][...], refs[1][...])


def _hgrn_in_kernel(*refs):
    *y_refs, g0_ref, sh_ref, sc_ref, w_ref, o_ref, h_ref = refs

    @pl.when(pl.program_id(1) == 0)
    def _():
        y = _stream_tile(y_refs)
        h_ref[...] = (_rms(y, g0_ref[0]) * (1.0 + sc_ref[0, 0]) + sh_ref[0, 0]).astype(BF16)

    acc = _dot(h_ref[...], w_ref[0])
    for c in range(HGRN_HEADS):
        o_ref[0, c] = acc[:, c * HGRN_HD:(c + 1) * HGRN_HD].astype(BF16)


def _hgrn_in(ys, norm_l, mods_l, w, j):
    return pl.pallas_call(
        _hgrn_in_kernel,
        grid=(NT_TOK // TM, 5),
        in_specs=_stream_specs(len(ys) == 2) + [
            _norm_spec(0), _mod_spec(0), _mod_spec(1),
            pl.BlockSpec((1, D_MODEL, D_MODEL), lambda i, sec: (j, 0, sec))],
        out_specs=pl.BlockSpec((1, HGRN_HEADS, TM, HGRN_HD), lambda i, sec: (sec, 0, i, 0)),
        out_shape=jax.ShapeDtypeStruct((5, HGRN_HEADS, NT_TOK, HGRN_HD), BF16),
        scratch_shapes=[pltpu.VMEM((TM, D_MODEL), BF16)],
        compiler_params=_cparams(("parallel", "arbitrary")),
        name="hgrn_in",
    )(*ys, norm_l, mods_l, mods_l, w)


HGRN_NBLK = NT_TOK // HGRN_BLOCK
HGRN_PBLK = NP_TOK // HGRN_BLOCK
HGRN_SBLK = SAMPLE_LEN // HGRN_BLOCK
HGRN_NSEQ = N_PROMPT_SEQ + N_SAMPLE_SEQ


def _hgrn_seq_of_block(blk):
    return jnp.where(blk < HGRN_PBLK, blk, HGRN_PBLK + (blk - HGRN_PBLK) // HGRN_SBLK)


def _hgrn_scan_kernel(zq_ref, zf_ref, zv_ref, lb_ref, s0_ref, o_ref, sfin_ref, st_ref, *, rev):
    c = HGRN_CHUNK
    nsub = HGRN_BLOCK // c
    i = pl.program_id(0)
    blk = (HGRN_NBLK - 1 - i) if rev else i
    is_prompt = blk < HGRN_PBLK
    pos = (blk - HGRN_PBLK) % HGRN_SBLK
    seq_start = jnp.logical_or(is_prompt, pos == (HGRN_SBLK - 1 if rev else 0))

    @pl.when(jnp.logical_and(seq_start, is_prompt))
    def _():
        st_ref[...] = jnp.zeros_like(st_ref)

    @pl.when(jnp.logical_and(seq_start, jnp.logical_not(is_prompt)))
    def _():
        def init(h, carry):
            st_ref[h] = s0_ref[0, h].T
            return carry
        lax.fori_loop(0, HGRN_HEADS, init, 0)

    row = lax.broadcasted_iota(jnp.int32, (c, HGRN_HD), 0)
    tt = lax.broadcasted_iota(jnp.int32, (c, c), 0)
    ss = lax.broadcasted_iota(jnp.int32, (c, c), 1)
    tri = ((ss >= tt) if rev else (ss <= tt)).astype(BF16)
    dsh = HGRN_DIAG.bit_length() - 1
    same_sub = (tt >> dsh) == (ss >> dsh)
    bands = [jnp.logical_and(same_sub, ss == ((tt + d) if rev else (tt - d))) for d in range(HGRN_DIAG)]
    levels = []
    size = 2 * HGRN_DIAG
    while size <= c:
        half = size // 2
        sh = size.bit_length() - 1
        row_hi = (row & (size - 1)) >= half
        t_hi = (tt & (size - 1)) >= half
        s_hi = (ss & (size - 1)) >= half
        if rev:
            is_query = jnp.logical_not(row_hi)
            pair = jnp.logical_and(jnp.logical_not(t_hi), s_hi)
        else:
            is_query = row_hi
            pair = jnp.logical_and(t_hi, jnp.logical_not(s_hi))
        pair = jnp.logical_and(pair, (tt >> sh) == (ss >> sh))
        levels.append((size, is_query, jnp.where(is_query, 1.0, -1.0), pair))
        size *= 2

    def shift1(x):
        x3 = x.reshape(c // SUBLANES, SUBLANES, HGRN_HD)
        return pltpu.roll(x3, (SUBLANES - 1) if rev else 1, 1).reshape(c, HGRN_HD)

    def chunk_math(zq, zf, v, lb, st):
        q = _silu(zq) * (HGRN_HD ** -0.5)
        f = lb + (1.0 - lb) * jax.nn.sigmoid(zf)
        k = 1.0 - f
        g = jnp.log(f)
        g1 = g.astype(BF16)
        g2 = (g - g1.astype(F32)).astype(BF16)
        cum = _dot(tri, g1) + _dot(tri, g2)
        tot = cum[0:1, :] if rev else cum[c - 1:c, :]

        z = k
        att = jnp.where(bands[0], jnp.sum(q * z, axis=-1, keepdims=True), 0.0)
        for d in range(1, HGRN_DIAG):
            z = shift1(z) * f
            att = jnp.where(bands[d], jnp.sum(q * z, axis=-1, keepdims=True), att)

        for size, is_query, sign, pair in levels:
            half = size // 2
            pieces = []
            for b in range(c // size):
                ref_row = b * size + (half if rev else half - 1)
                pieces.append(jnp.broadcast_to(cum[ref_row:ref_row + 1, :], (size, HGRN_HD)))
            ref = pieces[0] if len(pieces) == 1 else jnp.concatenate(pieces, axis=0)
            x = (jnp.where(is_query, q, k) * jnp.exp((cum - ref) * sign)).astype(BF16)
            att = jnp.where(pair, _dot_nt(x, x), att)

        o = _dot_nt((q * jnp.exp(cum)).astype(BF16), st.astype(BF16)) + _dot(att.astype(BF16), v.astype(BF16))

        ke = (k * jnp.exp(tot - cum)).astype(BF16)
        st_new = jnp.exp(tot) * st + _dot(v.T.astype(BF16), ke)
        return o.astype(BF16), st_new

    ngrp = HGRN_HEADS // HGRN_UNROLL

    def body(step, carry):
        ci = step // ngrp
        ci = (nsub - 1 - ci) if rev else ci
        h0 = (step % ngrp) * HGRN_UNROLL
        r0 = pl.multiple_of(ci * c, c)
        rows = pl.ds(r0, c)
        ins = []
        for u in range(HGRN_UNROLL):
            h = h0 + u
            ins.append((zq_ref[0, h, rows, :].astype(F32), zf_ref[0, h, rows, :].astype(F32),
                        zv_ref[0, h, rows, :].astype(F32), lb_ref[h], st_ref[h]))
        outs = [chunk_math(*a) for a in ins]
        for u in range(HGRN_UNROLL):
            o_ref[h0 + u, rows, :] = outs[u][0]
            st_ref[h0 + u] = outs[u][1]
        return carry

    lax.fori_loop(0, nsub * ngrp, body, 0)

    @pl.when(is_prompt)
    def _():
        def fin(h, carry):
            sfin_ref[0, h] = st_ref[h].T
            return carry
        lax.fori_loop(0, HGRN_HEADS, fin, 0)


def _hgrn_scan(p_hm, lb_d, s0_d, rev):
    def blk_of(i):
        return (HGRN_NBLK - 1 - i) if rev else i

    def sec_spec(sec):
        return pl.BlockSpec((1, HGRN_HEADS, HGRN_BLOCK, HGRN_HD), lambda i: (sec, 0, blk_of(i), 0))

    def seq_idx(i):
        return _hgrn_seq_of_block(blk_of(i))

    return pl.pallas_call(
        functools.partial(_hgrn_scan_kernel, rev=rev),
        grid=(HGRN_NBLK,),
        in_specs=[
            sec_spec(0), sec_spec(2 if rev else 1), sec_spec(3),
            pl.BlockSpec((HGRN_HEADS, 1, HGRN_HD), lambda i: (0, 0, 0)),
            pl.BlockSpec((1, HGRN_HEADS, HGRN_HD, HGRN_HD),
                         lambda i: (jnp.maximum(seq_idx(i) - N_PROMPT_SEQ, 0), 0, 0, 0)),
        ],
        out_specs=[
            pl.BlockSpec((HGRN_HEADS, HGRN_BLOCK, HGRN_HD), lambda i: (0, blk_of(i), 0)),
            pl.BlockSpec((1, HGRN_HEADS, HGRN_HD, HGRN_HD),
                         lambda i: (jnp.minimum(seq_idx(i), N_PROMPT_SEQ - 1), 0, 0, 0)),
        ],
        out_shape=[
            jax.ShapeDtypeStruct((HGRN_HEADS, NT_TOK, HGRN_HD), BF16),
            jax.ShapeDtypeStruct((N_PROMPT_SEQ, HGRN_HEADS, HGRN_HD, HGRN_HD), F32),
        ],
        scratch_shapes=[pltpu.VMEM((HGRN_HEADS, HGRN_HD, HGRN_HD), F32)],
        compiler_params=_cparams(("arbitrary",)),
        name="hgrn_scan_bwd" if rev else "hgrn_scan_fwd",
    )(p_hm, p_hm, p_hm, lb_d, s0_d)


def _hgrn_out_kernel(*refs):
    of_ref, ob_ref, zg_ref, ng_ref, *y_refs, g1_ref, gate_ref, w_ref, o_ref, x_ref = refs
    ng = ng_ref[...]
    for h in range(HGRN_HEADS):
        o = of_ref[h].astype(F32) + ob_ref[h].astype(F32)
        x_ref[:, h * HGRN_HD:(h + 1) * HGRN_HD] = (_rms(o, ng) * _silu(zg_ref[0, h].astype(F32))).astype(BF16)
    acc = _dot(x_ref[...], w_ref[0])
    o_ref[...] = _stream_tile(y_refs) + gate_ref[0, 0] * _rms(acc, g1_ref[0])


def _hgrn_out(o_f, o_b, p_hm, hgrn_norm_g, ys, norm_l, mods_l, w):
    head_spec = pl.BlockSpec((HGRN_HEADS, TM, HGRN_HD), lambda i: (0, i, 0))
    return pl.pallas_call(
        _hgrn_out_kernel,
        grid=(NT_TOK // TM,),
        in_specs=[
            head_spec, head_spec,
            pl.BlockSpec((1, HGRN_HEADS, TM, HGRN_HD), lambda i: (4, 0, i, 0)),
            pl.BlockSpec((1, HGRN_HD), lambda i: (0, 0))] + _stream_specs(len(ys) == 2) + [
            _norm_spec(1), _mod_spec(2),
            _resident((1, D_MODEL, D_MODEL)),
        ],
        out_specs=_tok_spec(D_MODEL),
        out_shape=jax.ShapeDtypeStruct((NT_TOK, D_MODEL), F32),
        scratch_shapes=[pltpu.VMEM((TM, D_MODEL), BF16)],
        compiler_params=_cparams(("parallel",)),
        name="hgrn_out",
    )(o_f, o_b, p_hm, hgrn_norm_g, *ys, norm_l, mods_l, w)


def _mla_down_kernel(y_ref, g0_ref, sh_ref, sc_ref, w_ref, qg_ref, kvg_ref, cos_ref, sin_ref,
                     cq_ref, ckv_ref, kpe_ref, kpe128_ref):
    p = _dot(_prenorm(y_ref, g0_ref, sh_ref, sc_ref), w_ref[...])
    cq_ref[...] = _rms(p[:, :MLA_Q_LORA], qg_ref[...]).astype(BF16)
    ckv_ref[...] = _rms(p[:, MLA_Q_LORA:MLA_Q_LORA + MLA_KV_LORA], kvg_ref[...])
    tail = p[:, MLA_Q_LORA + MLA_KV_LORA:]
    kpe_ref[...] = tail[:, :MLA_ROPE]
    kpe128_ref[...] = _rope128(tail, cos_ref[...], sin_ref[...]).astype(BF16)


def _mla_down(y, norm_l, mods_l, w, qg, kvg, cos_t, sin_t):
    return pl.pallas_call(
        _mla_down_kernel,
        grid=(NT_TOK // TM,),
        in_specs=[_tok_spec(D_MODEL), _norm_spec(0), _mod_spec(0), _mod_spec(1),
                  _resident((D_MODEL, MLA_DOWN_PAD)),
                  pl.BlockSpec((1, MLA_Q_LORA), lambda i: (0, 0)),
                  pl.BlockSpec((1, MLA_KV_LORA), lambda i: (0, 0)),
                  _rope_spec(), _rope_spec()],
        out_specs=[_tok_spec(MLA_Q_LORA), _tok_spec(MLA_KV_LORA), _tok_spec(MLA_ROPE), _tok_spec(128)],
        out_shape=[
            jax.ShapeDtypeStruct((NT_TOK, MLA_Q_LORA), BF16),
            jax.ShapeDtypeStruct((NT_TOK, MLA_KV_LORA), F32),
            jax.ShapeDtypeStruct((NT_TOK, MLA_ROPE), F32),
            jax.ShapeDtypeStruct((NT_TOK, 128), BF16),
        ],
        compiler_params=_cparams(("parallel",)),
        name="mla_down",
    )(y, norm_l, mods_l, mods_l, w, qg, kvg, cos_t, sin_t)


MLA_HB = 8


def _mla_q_kernel(cq_ref, w_ref, cos_ref, sin_ref, q_ref):
    acc = _dot(cq_ref[...], w_ref[0]) * MLA_Q_SCALE
    cos = cos_ref[...]
    sin = sin_ref[...]
    for hh in range(MLA_HB):
        c0 = hh * MLA_QK_PAD
        q_ref[hh, :, :MLA_NOPE] = acc[:, c0:c0 + MLA_NOPE].astype(BF16)
        q_ref[hh, :, MLA_NOPE:] = _rope128(acc[:, c0 + MLA_NOPE:c0 + MLA_QK_PAD], cos, sin).astype(BF16)


def _mla_q(cq, w_uq_g, cos_t, sin_t):
    return pl.pallas_call(
        _mla_q_kernel,
        grid=(NT_TOK // TM, MLA_HEADS // MLA_HB),
        in_specs=[_tok_spec(MLA_Q_LORA),
                  pl.BlockSpec((1, MLA_Q_LORA, MLA_HB * MLA_QK_PAD), lambda i, h: (h, 0, 0)),
                  _rope_spec(), _rope_spec()],
        out_specs=pl.BlockSpec((MLA_HB, TM, MLA_QK_PAD), lambda i, h: (h, i, 0)),
        out_shape=jax.ShapeDtypeStruct((MLA_HEADS, NT_TOK, MLA_QK_PAD), BF16),
        compiler_params=_cparams(("parallel", "arbitrary")),
        name="mla_q",
    )(cq, w_uq_g, cos_t, sin_t)


def _mla_kv_kernel(ckv_ref, kpe_ref, w_ref, k_ref, vt_ref):
    acc = _dot(ckv_ref[...], w_ref[0])
    kpe = kpe_ref[...]
    for hh in range(MLA_HB):
        c0 = hh * (MLA_NOPE + MLA_V)
        k_ref[hh, :, :MLA_NOPE] = acc[:, c0:c0 + MLA_NOPE].astype(BF16)
        k_ref[hh, :, MLA_NOPE:] = kpe
        vt_ref[hh] = acc[:, c0 + MLA_NOPE:c0 + MLA_NOPE + MLA_V].T.astype(BF16)


def _mla_kv(ckv_all, kpe_all, w_ukv_g):
    n = ckv_all.shape[0]
    return pl.pallas_call(
        _mla_kv_kernel,
        grid=(n // TM, MLA_HEADS // MLA_HB),
        in_specs=[_tok_spec(MLA_KV_LORA), _tok_spec(128),
                  pl.BlockSpec((1, MLA_KV_LORA, MLA_HB * (MLA_NOPE + MLA_V)), lambda i, h: (h, 0, 0))],
        out_specs=[pl.BlockSpec((MLA_HB, TM, MLA_QK_PAD), lambda i, h: (h, i, 0)),
                   pl.BlockSpec((MLA_HB, MLA_V, TM), lambda i, h: (h, 0, i))],
        out_shape=[jax.ShapeDtypeStruct((MLA_HEADS, n, MLA_QK_PAD), BF16),
                   jax.ShapeDtypeStruct((MLA_HEADS, MLA_V, n), BF16)],
        compiler_params=_cparams(("parallel", "arbitrary")),
        name="mla_kv",
    )(ckv_all, kpe_all, w_ukv_g)


def _row_partial(x, reduce_fn, combine_fn):
    rows, n = x.shape
    x3 = x.reshape(rows // 8, 8, n)
    parts = min(ROW_REDUCE_PARTS, rows // 8)
    bounds = [i * (rows // 8) // parts for i in range(parts + 1)]
    acc = [reduce_fn(x3[lo:hi], axis=0) for lo, hi in zip(bounds[:-1], bounds[1:])]
    while len(acc) > 1:
        acc = [combine_fn(a, b) for a, b in zip(acc[0::2], acc[1::2])] + ([acc[-1]] if len(acc) % 2 else [])
    return acc[0]


MLA_PROMPT_HB = 4


def _mla_attn_prompt_kernel(q_ref, k_ref, vt_ref, o_ref):
    for u in range(MLA_PROMPT_HB):
        s = _dot_nt(k_ref[u], q_ref[u])
        m = jnp.max(_row_partial(s, jnp.max, jnp.maximum), axis=0, keepdims=True)
        p = jnp.exp2(s - m)
        den = jnp.sum(_row_partial(p, jnp.sum, jnp.add), axis=0, keepdims=True)
        acc = _dot(vt_ref[u], p.astype(BF16))
        o_ref[:, u * MLA_V:(u + 1) * MLA_V] = (acc / den).T.astype(BF16)


MLA_TQ = 256
ROW_REDUCE_PARTS = 4


def _mla_attn_pipe_kernel(q_ref, k_ref, vt_ref, o_ref, s_ref):
    tq = MLA_TQ
    nt = SAMPLE_LEN // tq
    chunks = list(zip(MLA_KEY_CHUNKS[:-1], MLA_KEY_CHUNKS[1:]))

    def tile_rows(t):
        return pl.ds(t * tq, tq) if isinstance(t, int) else pl.ds(pl.multiple_of(t * tq, tq), tq)

    def stage(t_a, slot_a, t_b, slot_b, m_b):
        if t_a is not None:
            q = q_ref[0, tile_rows(t_a), :]
        m_parts = []
        den = acc = None
        for lo, hi in chunks:
            if t_a is not None:
                s = _dot_nt(k_ref[0, lo:hi, :], q)
                s_ref[slot_a, lo:hi, :] = s
                m_parts.append(_row_partial(s, jnp.max, jnp.maximum))
            if t_b is not None:
                p = jnp.exp2(s_ref[slot_b, lo:hi, :] - m_b)
                d = _row_partial(p, jnp.sum, jnp.add)
                a = _dot(vt_ref[0, :, lo:hi], p.astype(BF16))
                den = d if den is None else den + d
                acc = a if acc is None else acc + a
        if t_b is not None:
            den = jnp.sum(den, axis=0, keepdims=True)
            o_ref[tile_rows(t_b), :] = (acc / den).T.astype(BF16)
        if t_a is None:
            return None
        m_a = m_parts[0]
        for part in m_parts[1:]:
            m_a = jnp.maximum(m_a, part)
        return jnp.max(m_a, axis=0, keepdims=True)

    m = stage(0, 0, None, None, None)

    def body(j, m_prev):
        t = 2 * j + 1
        m_mid = stage(t, 1, t - 1, 0, m_prev)
        return stage(t + 1, 0, t, 1, m_mid)

    m = lax.fori_loop(0, (nt - 2) // 2, body, m)
    m = stage(nt - 1, 1, nt - 2, 0, m)
    stage(None, None, nt - 1, 1, m)


def _mla_attn(q_hm, k_hm, vt_hm):
    pk0 = N_SAMPLE_SEQ * MLA_KV_LEN // PROMPT_LEN
    hb = MLA_PROMPT_HB
    o_p = pl.pallas_call(
        _mla_attn_prompt_kernel,
        grid=(MLA_HEADS // hb, N_PROMPT_SEQ),
        in_specs=[pl.BlockSpec((hb, PROMPT_LEN, MLA_QK_PAD), lambda h, b: (h, b, 0)),
                  pl.BlockSpec((hb, PROMPT_LEN, MLA_QK_PAD), lambda h, b: (h, pk0 + b, 0)),
                  pl.BlockSpec((hb, MLA_V, PROMPT_LEN), lambda h, b: (h, 0, pk0 + b))],
        out_specs=pl.BlockSpec((PROMPT_LEN, hb * MLA_V), lambda h, b: (b, h)),
        out_shape=jax.ShapeDtypeStruct((NP_TOK, MLA_HEADS * MLA_V), BF16),
        compiler_params=_cparams(("parallel", "parallel")),
        name="mla_attn_prompt",
    )(q_hm, k_hm, vt_hm)
    q0 = NP_TOK // SAMPLE_LEN
    o_s = pl.pallas_call(
        _mla_attn_pipe_kernel,
        grid=(MLA_HEADS, N_SAMPLE_SEQ),
        in_specs=[pl.BlockSpec((1, SAMPLE_LEN, MLA_QK_PAD), lambda h, b: (h, q0 + b, 0)),
                  pl.BlockSpec((1, MLA_KV_LEN, MLA_QK_PAD), lambda h, b: (h, b, 0)),
                  pl.BlockSpec((1, MLA_V, MLA_KV_LEN), lambda h, b: (h, 0, b))],
        out_specs=pl.BlockSpec((SAMPLE_LEN, MLA_V), lambda h, b: (b, h)),
        out_shape=jax.ShapeDtypeStruct((NS_TOK, MLA_HEADS * MLA_V), BF16),
        scratch_shapes=[pltpu.VMEM((2, MLA_KV_LEN, MLA_TQ), F32)],
        compiler_params=_cparams(("parallel", "parallel")),
        name="mla_attn_sample",
    )(q_hm, k_hm, vt_hm)
    return o_p, o_s


def _swa_qkv_kernel(y_ref, g0_ref, sh_ref, sc_ref, w_ref, cos_ref, sin_ref, q_ref, kv_ref, kf_ref, vf_ref):
    p = _dot(_prenorm(y_ref, g0_ref, sh_ref, sc_ref), w_ref[...])
    cos = cos_ref[...]
    sin = sin_ref[...]
    kf_ref[...] = p[:, SWA_QW:SWA_QW + SWA_KVW]
    vf_ref[...] = p[:, SWA_QW + SWA_KVW:]
    for c in range(SWA_QW // 128):
        r = _rope128(p[:, c * 128:(c + 1) * 128], cos, sin).astype(BF16)
        q_ref[2 * c] = r[:, :SWA_HD]
        q_ref[2 * c + 1] = r[:, SWA_HD:]
    for c in range(SWA_KVW // 128):
        k0 = SWA_QW + c * 128
        v0 = SWA_QW + SWA_KVW + c * 128
        k = _rope128(p[:, k0:k0 + 128], cos, sin).astype(BF16)
        v = p[:, v0:v0 + 128].astype(BF16)
        kv_ref[2 * c] = jnp.concatenate([k[:, :SWA_HD], v[:, :SWA_HD]], axis=1)
        kv_ref[2 * c + 1] = jnp.concatenate([k[:, SWA_HD:], v[:, SWA_HD:]], axis=1)


def _swa_qkv(y, norm_l, mods_l, w, cos_t, sin_t):
    return pl.pallas_call(
        _swa_qkv_kernel,
        grid=(NT_TOK // TM,),
        in_specs=[_tok_spec(D_MODEL), _norm_spec(0), _mod_spec(0), _mod_spec(1),
                  _resident((D_MODEL, SWA_QW + 2 * SWA_KVW)), _rope_spec(), _rope_spec()],
        out_specs=[pl.BlockSpec((SWA_Q_HEADS, TM, SWA_HD), lambda i: (0, i, 0)),
                   pl.BlockSpec((SWA_KV_HEADS, TM, 2 * SWA_HD), lambda i: (0, i, 0)),
                   _tok_spec(SWA_KVW), _tok_spec(SWA_KVW)],
        out_shape=[
            jax.ShapeDtypeStruct((SWA_Q_HEADS, NT_TOK, SWA_HD), BF16),
            jax.ShapeDtypeStruct((SWA_KV_HEADS, NT_TOK, 2 * SWA_HD), BF16),
            jax.ShapeDtypeStruct((NT_TOK, SWA_KVW), F32),
            jax.ShapeDtypeStruct((NT_TOK, SWA_KVW), F32),
        ],
        compiler_params=_cparams(("parallel",)),
        name="swa_qkv",
    )(y, norm_l, mods_l, mods_l, w, cos_t, sin_t)


def _swa_attn_kernel(sink_ref, q_ref, kvc_ref, *rest, local):
    o_ref = rest[-1]
    kv0 = pl.program_id(0) * SWA_KVB
    tq = q_ref.shape[1]
    nq = SWA_GROUP * tq
    kvc = kvc_ref[...].reshape(SWA_KVB, kvc_ref.shape[-2], 2 * SWA_HD)
    head = lax.broadcasted_iota(jnp.int32, (1, nq), 1) >> (tq.bit_length() - 1)
    if local:
        j = pl.program_id(2)
        nb = pl.num_programs(2)
        key = lax.broadcasted_iota(jnp.int32, (SWA_BLOCK, nq), 0)
        qry = lax.broadcasted_iota(jnp.int32, (SWA_BLOCK, nq), 1) & (SWA_BLOCK - 1)
        keep = [jnp.logical_and(key >= qry, j > 0), None, jnp.logical_and(key <= qry, j < nb - 1)]
    for u in range(SWA_KVB):
        q = q_ref[u * SWA_GROUP:(u + 1) * SWA_GROUP].reshape(nq, SWA_HD) * (SWA_HD ** -0.5)
        sink = jnp.zeros((1, nq), F32)
        for g in range(SWA_GROUP):
            sink = jnp.where(head == g, sink_ref[(kv0 + u) * SWA_GROUP + g], sink)
        blocks = [kvc[u]]
        scores = [_dot_nt(blocks[0][:, :SWA_HD], q)]
        if local:
            for t in range(3):
                blk = rest[t][u]
                s = _dot_nt(blk[:, :SWA_HD], q)
                scores.append(s if keep[t] is None else jnp.where(keep[t], s, NEG_INF))
                blocks.append(blk)
        s = scores[0] if len(scores) == 1 else jnp.concatenate(scores, axis=0)
        kv_all = blocks[0] if len(blocks) == 1 else jnp.concatenate(blocks, axis=0)
        m = jnp.maximum(jnp.max(_row_partial(s, jnp.max, jnp.maximum), axis=0, keepdims=True), sink)
        p = jnp.exp(s - m)
        den = jnp.sum(_row_partial(p, jnp.sum, jnp.add), axis=0, keepdims=True) + jnp.exp(sink - m)
        kv_t = kv_all.astype(F32).T.astype(BF16)
        out = (_dot(kv_t, p.astype(BF16)) / den).T.astype(BF16)
        for g in range(SWA_GROUP):
            c0 = (u * SWA_GROUP + g) * SWA_HD
            o_ref[:, c0:c0 + SWA_HD] = out[g * tq:(g + 1) * tq, SWA_HD:]


def _swa_attn(sink, q_hm, kv_hm, kvc_s):
    smem = pl.BlockSpec(memory_space=pltpu.SMEM)
    gw = SWA_KVB * SWA_GROUP * SWA_HD
    qb = SWA_KVB * SWA_GROUP
    o_p = pl.pallas_call(
        functools.partial(_swa_attn_kernel, local=False),
        grid=(SWA_KV_HEADS // SWA_KVB, N_PROMPT_SEQ),
        in_specs=[smem,
                  pl.BlockSpec((qb, PROMPT_LEN, SWA_HD), lambda kv, b: (kv, b, 0)),
                  pl.BlockSpec((SWA_KVB, PROMPT_LEN, 2 * SWA_HD), lambda kv, b: (kv, b, 0))],
        out_specs=pl.BlockSpec((PROMPT_LEN, gw), lambda kv, b: (b, kv)),
        out_shape=jax.ShapeDtypeStruct((NP_TOK, SWA_QW), BF16),
        compiler_params=_cparams(("parallel", "parallel")),
        name="swa_attn_prompt",
    )(sink, q_hm, kv_hm)
    nb = SAMPLE_LEN // SWA_BLOCK
    b0 = NP_TOK // SWA_BLOCK

    def loc(off):
        return pl.BlockSpec((SWA_KVB, SWA_BLOCK, 2 * SWA_HD),
                            lambda kv, b, j: (kv, b0 + b * nb + jnp.clip(j + off, 0, nb - 1), 0))

    o_s = pl.pallas_call(
        functools.partial(_swa_attn_kernel, local=True),
        grid=(SWA_KV_HEADS // SWA_KVB, N_SAMPLE_SEQ, nb),
        in_specs=[smem,
                  pl.BlockSpec((qb, SWA_BLOCK, SWA_HD), lambda kv, b, j: (kv, b0 + b * nb + j, 0)),
                  pl.BlockSpec((1, SWA_KVB, PAST_LEN, 2 * SWA_HD), lambda kv, b, j: (b, kv, 0, 0)),
                  loc(-1), loc(0), loc(1)],
        out_specs=pl.BlockSpec((SWA_BLOCK, gw), lambda kv, b, j: (b * nb + j, kv)),
        out_shape=jax.ShapeDtypeStruct((NS_TOK, SWA_QW), BF16),
        compiler_params=_cparams(("parallel", "parallel", "arbitrary")),
        name="swa_attn_sample",
    )(sink, q_hm, kvc_s, kv_hm, kv_hm, kv_hm)
    return o_p, o_s


def _rope_tables():
    t = jnp.arange(SAMPLE_LEN, dtype=jnp.int32)
    inv = ROPE_BASE ** (-jnp.arange(16, dtype=F32) / 16)
    ang_r = (t // GRID_W).astype(F32)[:, None] * inv[None, :]
    ang_c = (t % GRID_W).astype(F32)[:, None] * inv[None, :]
    cos = jnp.concatenate([jnp.cos(ang_r), jnp.cos(ang_r), jnp.cos(ang_c), jnp.cos(ang_c)], axis=-1)
    sin = jnp.concatenate([-jnp.sin(ang_r), jnp.sin(ang_r), -jnp.sin(ang_c), jnp.sin(ang_c)], axis=-1)
    cos = jnp.concatenate([cos, cos], axis=-1)
    sin = jnp.concatenate([sin, sin], axis=-1)
    cos_t = jnp.concatenate([jnp.ones((NP_TOK, 128), F32), cos], axis=0)
    sin_t = jnp.concatenate([jnp.zeros((NP_TOK, 128), F32), sin], axis=0)
    return cos_t, sin_t


def kernel(x_prompt, x_sample, c, state_hgrn, cache_mla_ckv, cache_mla_kpe, cache_swa_k, cache_swa_v, c_ctx, ada_w, ada_b, norm_g, mlp_w_in, mlp_w_out, hgrn_w_in, hgrn_lb_logits, hgrn_norm_g, hgrn_w_out, mla_w_down, mla_q_norm_g, mla_kv_norm_g, mla_w_uq, mla_w_ukv, mla_w_out, swa_w_qkv, swa_sink, swa_w_out):
    ys = (x_prompt.reshape(NP_TOK, D_MODEL), x_sample.reshape(NS_TOK, D_MODEL))
    cond8 = jnp.concatenate([c_ctx[None, :], c, jnp.zeros((3, D_MODEL), F32)], axis=0)
    mods = _adaln(cond8, ada_w, ada_b).reshape(DEPTH, 8, N_MOD, 1, D_MODEL)
    cos_t, sin_t = _rope_tables()

    lb_soft = jax.nn.softmax(hgrn_lb_logits.astype(F32), axis=1)
    lb_all = jnp.cumsum(lb_soft, axis=1) - lb_soft[:, :1]

    w1_all = mlp_w_in.astype(BF16)
    w2_all = mlp_w_out.astype(BF16)
    hgrn_w_in_all = hgrn_w_in.astype(BF16)
    hgrn_w_out_all = hgrn_w_out.astype(BF16)

    new_hgrn = []
    new_ckv = new_kpe = new_k = new_v = None
    for layer in range(DEPTH):
        kind, j = layer % 3, layer // 3
        norm_l = norm_g[layer].reshape(4, 1, D_MODEL)
        mods_l = mods[layer]
        if kind == 0:
            if layer > 0:
                ys = (y,)
            p_hm = _hgrn_in(ys, norm_l, mods_l, hgrn_w_in_all, j)
            lb = lb_all[:, layer].reshape(2, HGRN_HEADS, 1, HGRN_HD)
            o_f, s_f = _hgrn_scan(p_hm, lb[0], state_hgrn[:, j, 0], rev=False)
            o_b, s_b = _hgrn_scan(p_hm, lb[1], state_hgrn[:, j, 1], rev=True)
            new_hgrn.append(jnp.stack([s_f, s_b], axis=1))
            y = _hgrn_out(o_f, o_b, p_hm, hgrn_norm_g[j].reshape(1, HGRN_HD), ys, norm_l, mods_l,
                          hgrn_w_out_all[j:j + 1])
        elif kind == 1:
            w_down = jnp.pad(mla_w_down[j], ((0, 0), (0, MLA_DOWN_PAD - mla_w_down.shape[-1]))).astype(BF16)
            cq, ckv, kpe, kpe128 = _mla_down(y, norm_l, mods_l, w_down, mla_q_norm_g[j].reshape(1, -1),
                                             mla_kv_norm_g[j].reshape(1, -1), cos_t, sin_t)
            ng = MLA_HEADS // MLA_HB
            w_uq = jnp.pad(mla_w_uq[j].reshape(MLA_Q_LORA, MLA_HEADS, MLA_QK),
                           ((0, 0), (0, 0), (0, MLA_QK_PAD - MLA_QK)))
            w_uq = w_uq.reshape(MLA_Q_LORA, ng, MLA_HB * MLA_QK_PAD).transpose(1, 0, 2).astype(BF16)
            w_ukv = mla_w_ukv[j].reshape(MLA_KV_LORA, ng, MLA_HB * (MLA_NOPE + MLA_V)).transpose(1, 0, 2).astype(BF16)
            q_hm = _mla_q(cq, w_uq, cos_t, sin_t)
            ckv_s = jnp.concatenate([cache_mla_ckv[:, j].astype(BF16),
                                     ckv[NP_TOK:].reshape(N_SAMPLE_SEQ, SAMPLE_LEN, MLA_KV_LORA).astype(BF16)], axis=1)
            kpe_ctx = jnp.pad(cache_mla_kpe[:, j], ((0, 0), (0, 0), (0, 128 - MLA_ROPE))).astype(BF16)
            kpe_s = jnp.concatenate([kpe_ctx, kpe128[NP_TOK:].reshape(N_SAMPLE_SEQ, SAMPLE_LEN, 128)], axis=1)
            ckv_all = jnp.concatenate([ckv_s.reshape(-1, MLA_KV_LORA), ckv[:NP_TOK].astype(BF16)], axis=0)
            kpe_all = jnp.concatenate([kpe_s.reshape(-1, 128), kpe128[:NP_TOK]], axis=0)
            k_hm, v_hm = _mla_kv(ckv_all, kpe_all, w_ukv)
            o_p, o_s = _mla_attn(q_hm, k_hm, v_hm)
            new_ckv = ckv[:NP_TOK].reshape(N_PROMPT_SEQ, 1, PROMPT_LEN, MLA_KV_LORA)
            new_kpe = kpe[:NP_TOK].reshape(N_PROMPT_SEQ, 1, PROMPT_LEN, MLA_ROPE)
            y = _outproj(o_p, o_s, y, norm_l, mods_l, mla_w_out[j].astype(BF16))
        else:
            q_hm, kv_hm, kf, vf = _swa_qkv(y, norm_l, mods_l, swa_w_qkv[j].astype(BF16), cos_t, sin_t)
            kvc_s = jnp.concatenate([cache_swa_k[:, j], cache_swa_v[:, j]], axis=-1).transpose(0, 2, 1, 3).astype(BF16)
            o_p, o_s = _swa_attn(swa_sink[j], q_hm, kv_hm, kvc_s)
            new_k = kf[:NP_TOK].reshape(N_PROMPT_SEQ, 1, PROMPT_LEN, SWA_KV_HEADS, SWA_HD)
            new_v = vf[:NP_TOK].reshape(N_PROMPT_SEQ, 1, PROMPT_LEN, SWA_KV_HEADS, SWA_HD)
            y = _outproj(o_p, o_s, y, norm_l, mods_l, swa_w_out[j].astype(BF16))
        if layer < DEPTH - 1:
            y = _mlp(y, norm_l, mods_l, w1_all, w2_all, layer, 0, NT_TOK // TM)
        else:
            y_prompt = _mlp(y, norm_l, mods_l, w1_all, w2_all, layer, 0, PROMPT_TILES)
            y_sample = _mlp(y, norm_l, mods_l, w1_all, w2_all, layer, PROMPT_TILES, NS_TOK // TM)

    return (y_prompt.reshape(N_PROMPT_SEQ, PROMPT_LEN, D_MODEL), y_sample.reshape(N_SAMPLE_SEQ, SAMPLE_LEN, D_MODEL),
            jnp.stack(new_hgrn, axis=1), new_ckv, new_kpe, new_k, new_v)
```
